```python
import jax, jax.numpy as jnp
from jax import lax
import numpy as np

D_MODEL = 2048
BATCH = 4
SEQ = 4096
DEPTH = 1

PLE_DIM = 256
RMS_EPS = 1e-6
M_HEADS = 8
M_DV = D_MODEL // M_HEADS
M_DK = M_DV // 2
M_QK_W = M_HEADS * M_DK
M_V_W = M_HEADS * M_DV
M_CHUNK = 64
CONV_W = 4
G_GROUPS = 8
G_W = D_MODEL
G_DG = G_W // G_GROUPS
G_CHUNK = 128
N_BRANCH = 2
IN_WIDTHS = (M_QK_W, M_QK_W, M_V_W, M_V_W, M_HEADS, M_HEADS, 2 * G_W, N_BRANCH * D_MODEL)
C_IN = sum(IN_WIDTHS)
N_EXPERTS = 32
TOP_K = 4
D_EXPERT = D_MODEL
SWIGLU_ALPHA = 1.702
SWIGLU_LIMIT = 7.0
MOE_BLOCK = 256

kernel_name = "hybrid_mlstm_sgu_moe_ple"


def rms_norm(x, g):
    xf = x.astype(jnp.float32)
    y = xf * lax.rsqrt(jnp.mean(xf * xf, axis=-1, keepdims=True) + RMS_EPS)
    return (y * g.astype(jnp.float32)).astype(x.dtype)


def causal_dwconv(u, w, b):
    T = u.shape[1]
    up = jnp.pad(u, ((0, 0), (CONV_W - 1, 0), (0, 0)))
    return b + sum(up[:, j:j + T] * w[j] for j in range(CONV_W))


def mlstm(q, k, v, i_pre, f_pre):
    B, T, H, DK = q.shape
    DV = v.shape[-1]
    nc = T // M_CHUNK
    f32 = jnp.float32

    def chunks(a):
        a = a.astype(f32).reshape((B, nc, M_CHUNK, H) + a.shape[3:])
        return jnp.moveaxis(a, (1, 3), (0, 2))

    qc = chunks(q)
    kc = chunks(k) * (DK ** -0.5)
    vc = chunks(v)
    lic = chunks(i_pre)
    lfc = chunks(jax.nn.log_sigmoid(f_pre.astype(f32)))
    causal = jnp.tril(jnp.ones((M_CHUNK, M_CHUNK), dtype=bool))

    def body(carry, inp):
        C, n, m = carry
        qj, kj, vj, li, lf = inp
        b = jnp.cumsum(lf, axis=-1)
        D = b[..., :, None] - b[..., None, :] + li[..., None, :]
        D = jnp.where(causal, D, -jnp.inf)
        m_inter = b + m[..., None]
        m_j = jnp.maximum(m_inter, jnp.max(D, axis=-1))
        s = jnp.einsum('bhld,bhsd->bhls', qj, kj) * jnp.exp(D - m_j[..., None])
        inter = jnp.exp(m_inter - m_j)
        num = jnp.einsum('bhls,bhsv->bhlv', s, vj) + inter[..., None] * jnp.einsum('bhld,bhdv->bhlv', qj, C)
        den = jnp.sum(s, axis=-1) + inter * jnp.einsum('bhld,bhd->bhl', qj, n)
        h = num / jnp.maximum(jnp.abs(den), jnp.exp(-m_j))[..., None]
        bL = b[..., -1]
        w_log = bL[..., None] - b + li
        m_new = jnp.maximum(bL + m, jnp.max(w_log, axis=-1))
        w = jnp.exp(w_log - m_new[..., None])
        decay = jnp.exp(bL + m - m_new)
        wk = w[..., None] * kj
        C_new = decay[..., None, None] * C + jnp.einsum('bhsd,bhsv->bhdv', wk, vj)
        n_new = decay[..., None] * n + jnp.sum(wk, axis=2)
        return (C_new, n_new, m_new), h

    init = (jnp.zeros((B, H, DK, DV), f32), jnp.zeros((B, H, DK), f32), jnp.zeros((B, H), f32))
    _, hc = lax.scan(body, init, (qc, kc, vc, lic, lfc))
    h = jnp.moveaxis(hc, (0, 2), (1, 3)).reshape(B, T, H, DV)
    return h.astype(q.dtype)


def spatial_gating(zg, g_sgu, w_s, b_s):
    B, T, _ = zg.shape
    z = jax.nn.gelu(zg)
    u, v = jnp.split(z, 2, axis=-1)
    v = rms_norm(v, g_sgu)
    v = v.reshape(B, T // G_CHUNK, G_CHUNK, G_GROUPS, G_DG)
    ws = jnp.tril(w_s)
    vm = jnp.einsum('gts,bcsgd->bctgd', ws, v) + jnp.transpose(b_s)[None, None, :, :, None]
    return u * vm.reshape(B, T, G_W)


def clamped_swiglu(gu):
    x_glu, x_lin = jnp.split(gu, 2, axis=-1)
    x_glu = jnp.minimum(x_glu, SWIGLU_LIMIT)
    x_lin = jnp.clip(x_lin, -SWIGLU_LIMIT, SWIGLU_LIMIT)
    return x_glu * jax.nn.sigmoid(SWIGLU_ALPHA * x_glu) * (x_lin + 1.0)


def moe(h, w_router, b_router, w_gu, b_gu, w_dn, b_dn):
    N, D = h.shape
    logits = (h @ w_router + b_router).astype(jnp.float32)
    top_val, top_idx = lax.top_k(logits, TOP_K)
    gates = jax.nn.softmax(top_val, axis=-1)
    A = N * TOP_K
    e_flat = top_idx.reshape(A)
    t_flat = jnp.repeat(jnp.arange(N, dtype=jnp.int32), TOP_K)
    g_flat = gates.reshape(A).astype(h.dtype)
    order = jnp.argsort(e_flat)
    e_s, t_s, g_s = e_flat[order], t_flat[order], g_flat[order]
    counts = jnp.bincount(e_flat, length=N_EXPERTS)
    padded = ((counts + MOE_BLOCK - 1) // MOE_BLOCK) * MOE_BLOCK
    start = jnp.cumsum(counts) - counts
    pend = jnp.cumsum(padded)
    pstart = pend - padded
    dest = pstart[e_s] + (jnp.arange(A, dtype=jnp.int32) - start[e_s])
    n_blocks = -(-A // MOE_BLOCK) + N_EXPERTS
    P = n_blocks * MOE_BLOCK
    tok_buf = jnp.zeros((P,), jnp.int32).at[dest].set(t_s)
    gate_buf = jnp.zeros((P,), h.dtype).at[dest].set(g_s)
    blk_start = jnp.arange(n_blocks, dtype=jnp.int32) * MOE_BLOCK
    blk_expert = jnp.minimum(jnp.searchsorted(pend, blk_start, side='right'), N_EXPERTS - 1)

    def expert_block(args):
        toks, e = args
        xb = h[toks]
        act = clamped_swiglu(xb @ w_gu[e] + b_gu[e])
        return act @ w_dn[e] + b_dn[e]

    y_buf = lax.map(expert_block, (tok_buf.reshape(n_blocks, MOE_BLOCK), blk_expert))
    y_buf = y_buf.reshape(P, D) * gate_buf[:, None]
    return jnp.zeros_like(h).at[tok_buf].add(y_buf)


def setup_inputs(seed: int = 0) -> dict:
    key = jax.random.key(seed)
    ks = jax.random.split(key, 32)
    nrm = jax.random.normal
    f32 = jnp.float32

    def gain(k, shape):
        return 1.0 + 0.02 * nrm(k, shape, f32)

    b_i = 0.1 * nrm(ks[6], (DEPTH, M_HEADS), f32)
    b_f = jnp.linspace(3.0, 6.0, M_HEADS, dtype=f32)[None, :] + 0.1 * nrm(ks[7], (DEPTH, M_HEADS), f32)
    return {
        "x": nrm(ks[0], (BATCH, SEQ, D_MODEL), f32),
        "p": nrm(ks[1], (DEPTH, BATCH, SEQ, PLE_DIM), f32),
        "g_mix": gain(ks[2], (DEPTH, D_MODEL)),
        "w_in": nrm(ks[3], (DEPTH, D_MODEL, C_IN), f32) * D_MODEL ** -0.5,
        "conv_w": nrm(ks[4], (DEPTH, CONV_W, 2 * M_QK_W), f32) * CONV_W ** -0.5,
        "conv_b": 0.01 * nrm(ks[5], (DEPTH, 2 * M_QK_W), f32),
        "b_if": jnp.stack([b_i, b_f], axis=1),
        "g_mh": gain(ks[8], (DEPTH, M_V_W)),
        "g_sgu": gain(ks[9], (DEPTH, G_W)),
        "w_s": nrm(ks[10], (DEPTH, G_GROUPS, G_CHUNK, G_CHUNK), f32) * G_CHUNK ** -0.5,
        "b_s": 1.0 + 0.1 * nrm(ks[11], (DEPTH, G_GROUPS, G_CHUNK), f32),
        "w_br": nrm(ks[12], (DEPTH, N_BRANCH, G_W, D_MODEL), f32) * G_W ** -0.5,
        "w_out": nrm(ks[13], (DEPTH, D_MODEL, D_MODEL), f32) * D_MODEL ** -0.5,
        "g_ffn": gain(ks[14], (DEPTH, D_MODEL)),
        "w_router": nrm(ks[15], (DEPTH, D_MODEL, N_EXPERTS), f32) * D_MODEL ** -0.5,
        "b_router": 0.01 * nrm(ks[16], (DEPTH, N_EXPERTS), f32),
        "w_gu": nrm(ks[17], (DEPTH, N_EXPERTS, D_MODEL, 2 * D_EXPERT), f32) * D_MODEL ** -0.5,
        "b_gu": 0.01 * nrm(ks[18], (DEPTH, N_EXPERTS, 2 * D_EXPERT), f32),
        "w_dn": nrm(ks[19], (DEPTH, N_EXPERTS, D_EXPERT, D_MODEL), f32) * D_EXPERT ** -0.5,
        "b_dn": 0.01 * nrm(ks[20], (DEPTH, N_EXPERTS, D_MODEL), f32),
        "g_ple": gain(ks[21], (DEPTH, D_MODEL)),
        "w_pg": nrm(ks[22], (DEPTH, D_MODEL, D_MODEL), f32) * D_MODEL ** -0.5,
        "w_ple": nrm(ks[23], (DEPTH, PLE_DIM, D_MODEL), f32) * PLE_DIM ** -0.5,
        "g_ple_post": gain(ks[24], (DEPTH, D_MODEL)),
        "g_final": gain(ks[25], (D_MODEL,)),
    }


def reference(x, p, g_mix, w_in, conv_w, conv_b, b_if, g_mh, g_sgu, w_s, b_s, w_br, w_out,
              g_ffn, w_router, b_router, w_gu, b_gu, w_dn, b_dn, g_ple, w_pg, w_ple,
              g_ple_post, g_final):
    B, T, D = x.shape
    split_at = np.cumsum(IN_WIDTHS)[:-1].tolist()
    for i in range(DEPTH):
        xn = rms_norm(x, g_mix[i])
        z = xn @ w_in[i]
        q, k, v, o_pre, i_pre, f_pre, zg, gate_pre = jnp.split(z, split_at, axis=-1)
        qk = jax.nn.silu(causal_dwconv(jnp.concatenate([q, k], axis=-1), conv_w[i], conv_b[i]))
        q, k = jnp.split(qk, 2, axis=-1)
        h_m = mlstm(q.reshape(B, T, M_HEADS, M_DK), k.reshape(B, T, M_HEADS, M_DK),
                    v.reshape(B, T, M_HEADS, M_DV), i_pre + b_if[i, 0], f_pre + b_if[i, 1])
        h_m = rms_norm(h_m, g_mh[i].reshape(M_HEADS, M_DV))
        y_a = h_m.reshape(B, T, M_V_W) * jax.nn.sigmoid(o_pre)
        y_b = spatial_gating(zg, g_sgu[i], w_s[i], b_s[i])
        br = jnp.einsum('btnw,nwd->btnd', jnp.stack([y_a, y_b], axis=2), w_br[i])
        g = jax.nn.sigmoid(gate_pre.reshape(B, T, N_BRANCH, D))
        x = x + jnp.sum(g * br, axis=2) @ w_out[i]
        hn = rms_norm(x, g_ffn[i]).reshape(B * T, D)
        x = x + moe(hn, w_router[i], b_router[i], w_gu[i], b_gu[i], w_dn[i], b_dn[i]).reshape(B, T, D)
        e = rms_norm(p[i] @ w_ple[i], g_ple_post[i])
        x = x + jax.nn.sigmoid(rms_norm(x, g_ple[i]) @ w_pg[i]) * e
    return rms_norm(x, g_final)
```

```python
import functools

import jax
import jax.numpy as jnp
from jax import lax
from jax.experimental import pallas as pl
from jax.experimental.pallas import tpu as pltpu

F32 = jnp.float32
BF16 = jnp.bfloat16
U32 = jnp.uint32
I32 = jnp.int32

RMS_EPS = 1e-6
M_HEADS = 8
CONV_W = 4
G_GROUPS = 8
G_CHUNK = 128
TOP_K = 4
SWIGLU_ALPHA = 1.702
SWIGLU_LIMIT = 7.0

LANES = 128
MLSTM_CHUNK = 256
MOE_SUB = 256
MOE_WIN = 1024
MOE_TF = 256
VMEM_LIMIT = 56 * 1024 * 1024


def _cparams(sem, vmem=VMEM_LIMIT):
    return pltpu.CompilerParams(dimension_semantics=sem, vmem_limit_bytes=vmem)


def _dot(a, b, **kw):
    return jnp.dot(a, b, preferred_element_type=F32, **kw)


def _rms(x, g):
    ms = jnp.mean(x * x, axis=-1, keepdims=True)
    return x * lax.rsqrt(ms + RMS_EPS) * g


def _sigmoid(x):
    return 1.0 / (1.0 + jnp.exp(-x))


def _pack_pair(hi, lo):
    hb = lax.bitcast_convert_type(hi.astype(BF16).astype(F32), U32)
    lb = lax.bitcast_convert_type(lo.astype(BF16).astype(F32), U32)
    return hb | (lb >> 16)


def _unpack_hi(w):
    return lax.bitcast_convert_type(w & jnp.uint32(0xFFFF0000), F32)


def _unpack_lo(w):
    return lax.bitcast_convert_type(w << 16, F32)


def _norm_kernel(x_ref, g_ref, wif_ref, bif_ref, xn_ref, zif_ref):
    xn = _rms(x_ref[...], g_ref[...]).astype(BF16)
    xn_ref[...] = xn
    zif_ref[...] = _dot(xn, wif_ref[...]) + bif_ref[...]


def _input_norm(x2, g, w_if, b_if, tm):
    n, d = x2.shape
    return pl.pallas_call(
        _norm_kernel,
        out_shape=(jax.ShapeDtypeStruct((n, d), BF16), jax.ShapeDtypeStruct((n, LANES), F32)),
        grid=(n // tm,),
        in_specs=[
            pl.BlockSpec((tm, d), lambda i: (i, 0)),
            pl.BlockSpec((1, d), lambda i: (0, 0)),
            pl.BlockSpec((d, LANES), lambda i: (0, 0)),
            pl.BlockSpec((1, LANES), lambda i: (0, 0)),
        ],
        out_specs=(pl.BlockSpec((tm, d), lambda i: (i, 0)), pl.BlockSpec((tm, LANES), lambda i: (i, 0))),
        compiler_params=_cparams(("arbitrary",)),
        name="input_norm",
    )(x2, g, w_if, b_if)


def _inproj_kernel(x_ref, w_ref, o_ref, *, n_plain, sig_tiles):
    acc = _dot(x_ref[...], w_ref[...])
    j = pl.program_id(0)
    is_sig = functools.reduce(jnp.logical_or, [j == t for t in sig_tiles])
    is_plain = j < n_plain

    @pl.when(is_plain)
    def _():
        o_ref[...] = acc.astype(o_ref.dtype)

    @pl.when(is_sig)
    def _():
        o_ref[...] = _sigmoid(acc).astype(o_ref.dtype)

    @pl.when(jnp.logical_not(jnp.logical_or(is_plain, is_sig)))
    def _():
        o_ref[...] = jax.nn.gelu(acc).astype(o_ref.dtype)


def _input_proj(xn, w_main, tm, tn, n_plain, sig_tiles):
    n, d = xn.shape
    c = w_main.shape[1]
    return pl.pallas_call(
        functools.partial(_inproj_kernel, n_plain=n_plain, sig_tiles=sig_tiles),
        out_shape=jax.ShapeDtypeStruct((n, c), BF16),
        grid=(c // tn, n // tm),
        in_specs=[
            pl.BlockSpec((tm, d), lambda j, i: (i, 0)),
            pl.BlockSpec((d, tn), lambda j, i: (0, j)),
        ],
        out_specs=pl.BlockSpec((tm, tn), lambda j, i: (i, j)),
        compiler_params=_cparams(("arbitrary", "arbitrary")),
        name="input_proj",
    )(xn, w_main)


def _log_sigmoid(x):
    return jnp.minimum(x, 0.0) - jnp.log1p(jnp.exp(-jnp.abs(x)))


def _cumsum_lanes(x):
    n = x.shape[1]
    lane = lax.broadcasted_iota(I32, x.shape, 1)
    sh = 1
    while sh < n:
        x = x + jnp.where(lane >= sh, pltpu.roll(x, sh, axis=1), 0.0)
        sh *= 2
    return x


def _mlstm_kernel(qk_ref, v_ref, o_ref, zr_ref, cw_ref, cb_ref, gmh_ref, y_ref,
                  ext_s, qk_s, c_s, m_s, g_s, *, L, H, DK, DV):
    @pl.when(pl.program_id(1) == 0)
    def _():
        ext_s[0:8, :] = jnp.zeros((8, ext_s.shape[1]), F32)
        c_s[...] = jnp.zeros(c_s.shape, F32)
        m_s[...] = jnp.zeros(m_s.shape, F32)
        g_s[...] = jnp.zeros(g_s.shape, F32)

    u = qk_ref[...].astype(F32)
    ext_s[8:8 + L, :] = u
    cw = cw_ref[...]
    conv = (cb_ref[...] + cw[3:4] * u + cw[2:3] * ext_s[7:7 + L, :]
            + cw[1:2] * ext_s[6:6 + L, :] + cw[0:1] * ext_s[5:5 + L, :])
    ext_s[0:8, :] = ext_s[L:L + 8, :]
    qk_s[...] = conv * _sigmoid(conv)

    zr = zr_ref[...]
    i_rows = zr[0:H]
    b_rows = _cumsum_lanes(_log_sigmoid(zr[H:2 * H]))
    g_s[0:H, :] = b_rows
    b_cols = g_s[...].T

    row = lax.broadcasted_iota(I32, (L, L), 0)
    col = lax.broadcasted_iota(I32, (L, L), 1)
    causal = row >= col
    ones_blk = jnp.where(lax.broadcasted_iota(I32, (L, LANES), 1) == 0, 1.0, 0.0).astype(BF16)

    for h in range(H):
        q = qk_s[:, h * DK:(h + 1) * DK]
        k = qk_s[:, (H + h) * DK:(H + h + 1) * DK] * (DK ** -0.5)
        kt = k.T
        b_col = b_cols[:, h:h + 1]
        b_row = b_rows[h:h + 1, :]
        i_row = i_rows[h:h + 1, :]
        m_prev = m_s[h:h + 1, 0:1]
        dmat = jnp.where(causal, b_col - b_row + i_row, -jnp.inf)
        m_inter = b_col + m_prev
        m_j = jnp.maximum(m_inter, jnp.max(dmat, axis=1, keepdims=True))
        q16 = q.astype(BF16)
        s = _dot(q16, kt.astype(BF16)) * jnp.exp(dmat - m_j)
        inter = jnp.exp(m_inter - m_j)
        vx = jnp.concatenate([v_ref[:, h * DV:(h + 1) * DV], ones_blk], axis=1)
        cx = c_s[h]
        nd = _dot(s.astype(BF16), vx) + inter * _dot(q16, cx.astype(BF16))
        den = nd[:, DV:DV + 1]
        hh = nd[:, 0:DV] / jnp.maximum(jnp.abs(den), jnp.exp(-m_j))
        b_last = b_row[:, L - 1:L]
        w_log = b_last - b_row + i_row
        m_new = jnp.maximum(b_last + m_prev, jnp.max(w_log, axis=1, keepdims=True))
        ktw = (kt * jnp.exp(w_log - m_new)).astype(BF16)
        c_s[h] = jnp.exp(b_last + m_prev - m_new) * cx + _dot(ktw, vx)
        m_s[h:h + 1, :] = jnp.broadcast_to(m_new, (1, LANES))
        yn = _rms(hh, gmh_ref[:, h * DV:(h + 1) * DV])
        y_ref[:, h * DV:(h + 1) * DV] = (yn * o_ref[:, h * DV:(h + 1) * DV].astype(F32)).astype(y_ref.dtype)


def _mlstm(z_main, zif_t, conv_w, conv_b, g_mh, batch, seq, d):
    L, H = MLSTM_CHUNK, M_HEADS
    DV = d // H
    DK = DV // 2
    nc = seq // L
    n = batch * seq
    row_blk = lambda b, c: b * nc + c
    return pl.pallas_call(
        functools.partial(_mlstm_kernel, L=L, H=H, DK=DK, DV=DV),
        out_shape=jax.ShapeDtypeStruct((n, d), BF16),
        grid=(batch, nc),
        in_specs=[
            pl.BlockSpec((L, d), lambda b, c: (row_blk(b, c), 0)),
            pl.BlockSpec((L, d), lambda b, c: (row_blk(b, c), 1)),
            pl.BlockSpec((L, d), lambda b, c: (row_blk(b, c), 2)),
            pl.BlockSpec((2 * H, L), lambda b, c: (0, row_blk(b, c))),
            pl.BlockSpec((CONV_W, d), lambda b, c: (0, 0)),
            pl.BlockSpec((1, d), lambda b, c: (0, 0)),
            pl.BlockSpec((1, d), lambda b, c: (0, 0)),
        ],
        out_specs=pl.BlockSpec((L, d), lambda b, c: (row_blk(b, c), 0)),
        scratch_shapes=[
            pltpu.VMEM((L + 8, d), F32),
            pltpu.VMEM((L, d), F32),
            pltpu.VMEM((H, DK, DV + LANES), F32),
            pltpu.VMEM((H, LANES), F32),
            pltpu.VMEM((LANES, L), F32),
        ],
        compiler_params=_cparams(("arbitrary", "arbitrary")),
        name="mlstm",
    )(z_main, z_main, z_main, zif_t, conv_w, conv_b, g_mh)


def _sgu_kernel(u_ref, v_ref, ws_ref, bst_ref, g_ref, y_ref, vn_s, *, R, DG):
    vn_s[...] = _rms(v_ref[...].astype(F32), g_ref[...]).astype(BF16)
    row = lax.broadcasted_iota(I32, (G_CHUNK, G_CHUNK), 0)
    col = lax.broadcasted_iota(I32, (G_CHUNK, G_CHUNK), 1)
    for g in range(G_GROUPS):
        ws = jnp.where(row >= col, ws_ref[g], 0.0).astype(BF16)
        bias = bst_ref[:, g:g + 1]
        for c in range(R // G_CHUNK):
            rs = slice(c * G_CHUNK, (c + 1) * G_CHUNK)
            cs = slice(g * DG, (g + 1) * DG)
            vm = _dot(ws, vn_s[rs, cs]) + bias
            y_ref[rs, cs] = (u_ref[rs, cs].astype(F32) * vm).astype(y_ref.dtype)


def _sgu(z_main, w_s, b_s_t, g_sgu, d, R):
    n = z_main.shape[0]
    return pl.pallas_call(
        functools.partial(_sgu_kernel, R=R, DG=d // G_GROUPS),
        out_shape=jax.ShapeDtypeStruct((n, d), BF16),
        grid=(n // R,),
        in_specs=[
            pl.BlockSpec((R, d), lambda i: (i, 3)),
            pl.BlockSpec((R, d), lambda i: (i, 4)),
            pl.BlockSpec((G_GROUPS, G_CHUNK, G_CHUNK), lambda i: (0, 0, 0)),
            pl.BlockSpec((G_CHUNK, G_GROUPS), lambda i: (0, 0)),
            pl.BlockSpec((1, d), lambda i: (0, 0)),
        ],
        out_specs=pl.BlockSpec((R, d), lambda i: (i, 0)),
        scratch_shapes=[pltpu.VMEM((R, d), BF16)],
        compiler_params=_cparams(("arbitrary",)),
        name="spatial_gating",
    )(z_main, z_main, w_s, b_s_t, g_sgu)


def _merge_kernel(ya_ref, yb_ref, ga_ref, gb_ref, w0_ref, w1_ref, o_ref):
    a = _dot(ya_ref[...], w0_ref[...])
    b = _dot(yb_ref[...], w1_ref[...])
    o_ref[...] = (ga_ref[...].astype(F32) * a + gb_ref[...].astype(F32) * b).astype(o_ref.dtype)


def _merge(y_a, y_b, z_main, w_br, d, tm, tn):
    n = y_a.shape[0]
    ga0 = 5 * d // tn
    gb0 = 6 * d // tn
    return pl.pallas_call(
        _merge_kernel,
        out_shape=jax.ShapeDtypeStruct((n, d), BF16),
        grid=(d // tn, n // tm),
        in_specs=[
            pl.BlockSpec((tm, d), lambda j, i: (i, 0)),
            pl.BlockSpec((tm, d), lambda j, i: (i, 0)),
            pl.BlockSpec((tm, tn), lambda j, i: (i, ga0 + j)),
            pl.BlockSpec((tm, tn), lambda j, i: (i, gb0 + j)),
            pl.BlockSpec((None, d, tn), lambda j, i: (0, 0, j)),
            pl.BlockSpec((None, d, tn), lambda j, i: (1, 0, j)),
        ],
        out_specs=pl.BlockSpec((tm, tn), lambda j, i: (i, j)),
        compiler_params=_cparams(("arbitrary", "arbitrary")),
        name="branch_merge",
    )(y_a, y_b, z_main, z_main, w_br, w_br)


def _outproj_kernel(x_ref, mix_ref, wo_ref, gf_ref, wr_ref, br_ref,
                    x1_ref, hp_ref, idx_ref, gate_ref, *, E):
    x1 = x_ref[...] + _dot(mix_ref[...], wo_ref[...])
    x1_ref[...] = x1
    hn = _rms(x1, gf_ref[...])
    half = hn.shape[1] // 2
    hp_ref[...] = _pack_pair(hn[:, :half], hn[:, half:])
    logits = _dot(hn, wr_ref[...], precision=lax.Precision.HIGHEST) + br_ref[...]
    lane = lax.broadcasted_iota(I32, logits.shape, 1)
    work = jnp.where(lane < E, logits, -jnp.inf)
    vals, idxs = [], []
    for _ in range(TOP_K):
        mx = jnp.max(work, axis=1, keepdims=True)
        ix = jnp.min(jnp.where(work == mx, lane, LANES), axis=1, keepdims=True)
        vals.append(mx)
        idxs.append(ix)
        work = jnp.where(lane == ix, -jnp.inf, work)
    exps = [jnp.exp(v - vals[0]) for v in vals]
    inv = 1.0 / functools.reduce(jnp.add, exps)
    idx_out = jnp.zeros(logits.shape, I32)
    gate_out = jnp.zeros(logits.shape, F32)
    for k in range(TOP_K):
        idx_out = jnp.where(lane == k, idxs[k], idx_out)
        gate_out = jnp.where(lane == k, exps[k] * inv, gate_out)
    idx_ref[...] = idx_out
    gate_ref[...] = gate_out


def _out_proj(x2, mix, w_out, g_ffn, w_router, b_router, E, tm):
    n, d = x2.shape
    return pl.pallas_call(
        functools.partial(_outproj_kernel, E=E),
        out_shape=(
            jax.ShapeDtypeStruct((n, d), F32),
            jax.ShapeDtypeStruct((n, d // 2), U32),
            jax.ShapeDtypeStruct((n, LANES), I32),
            jax.ShapeDtypeStruct((n, LANES), F32),
        ),
        grid=(n // tm,),
        in_specs=[
            pl.BlockSpec((tm, d), lambda i: (i, 0)),
            pl.BlockSpec((tm, d), lambda i: (i, 0)),
            pl.BlockSpec((d, d), lambda i: (0, 0)),
            pl.BlockSpec((1, d), lambda i: (0, 0)),
            pl.BlockSpec((d, LANES), lambda i: (0, 0)),
            pl.BlockSpec((1, LANES), lambda i: (0, 0)),
        ],
        out_specs=(
            pl.BlockSpec((tm, d), lambda i: (i, 0)),
            pl.BlockSpec((tm, d // 2), lambda i: (i, 0)),
            pl.BlockSpec((tm, LANES), lambda i: (i, 0)),
            pl.BlockSpec((tm, LANES), lambda i: (i, 0)),
        ),
        compiler_params=_cparams(("arbitrary",)),
        name="out_proj_router",
    )(x2, mix, w_out, g_ffn, w_router, b_router)


def _route_plan(top_idx, E, P):
    n, k = top_idx.shape
    a = n * k
    e_flat = top_idx.reshape(a)
    onehot = (e_flat[:, None] == jnp.arange(E, dtype=I32)[None, :]).astype(I32)
    csum = jnp.cumsum(onehot, axis=0)
    rank = jnp.take_along_axis(csum, e_flat[:, None], axis=1)[:, 0] - 1
    counts = csum[-1]
    padded = ((counts + MOE_SUB - 1) // MOE_SUB) * MOE_SUB
    pend = jnp.cumsum(padded)
    pstart = pend - padded
    dest = (pstart[e_flat] + rank).astype(I32)

    nsb = P // MOE_SUB
    spw = MOE_WIN // MOE_SUB
    ni = P // MOE_WIN + E
    sb = jnp.arange(nsb, dtype=I32)
    sb_e = jnp.minimum(jnp.searchsorted(pend, sb * MOE_SUB, side="right"), E - 1).astype(I32)
    valid = sb * MOE_SUB < pend[-1]
    sb_e = jnp.where(valid, sb_e, sb_e[pend[-1] // MOE_SUB - 1])
    prev_e = jnp.concatenate([jnp.full((1,), -1, I32), sb_e[:-1]])
    is_start = (sb % spw == 0) | (sb_e != prev_e)
    item_of_sb = jnp.cumsum(is_start.astype(I32)) - 1
    n_items = jnp.sum(is_start.astype(I32))
    slot = jnp.where(is_start, item_of_sb, ni)
    item_win = jnp.zeros((ni,), I32).at[slot].set(sb // spw, mode="drop")
    item_e = jnp.zeros((ni,), I32).at[slot].set(sb_e, mode="drop")
    item_lo = jnp.zeros((ni,), I32).at[slot].set(sb % spw, mode="drop")
    item_first = jnp.zeros((ni,), I32).at[slot].set((sb % spw == 0).astype(I32), mode="drop")
    item_len = jnp.zeros((ni,), I32).at[jnp.where(valid, item_of_sb, ni)].add(1, mode="drop")
    it = jnp.arange(ni, dtype=I32)
    live = it < n_items
    last = jnp.maximum(n_items - 1, 0)
    src = jnp.minimum(it, last)
    item_win = item_win[src]
    item_e = item_e[src]
    item_lo = jnp.where(live, item_lo, 0)
    item_hi = jnp.where(live, item_lo + item_len, 0)
    item_first = jnp.where(live, item_first, 0)
    return dest, (item_win, item_e, item_lo, item_hi, item_first)


def _dispatch_kernel(dest_ref, hp_ref, xs_in_ref, xs_ref, sem, *, G):
    del xs_in_ref
    base = pl.program_id(0) * G

    def issue(r, carry):
        for k in range(TOP_K):
            d = dest_ref[(base + r) * TOP_K + k]
            pltpu.make_async_copy(hp_ref.at[pl.ds(r, 1)], xs_ref.at[pl.ds(d, 1)], sem).start()
        return carry

    lax.fori_loop(0, G, issue, 0)
    pltpu.make_async_copy(xs_ref.at[pl.ds(0, G * TOP_K)], xs_ref.at[pl.ds(0, G * TOP_K)], sem).wait()


def _dispatch(dest, hp, P, G):
    n, w = hp.shape
    xs_init = jnp.zeros((P, w), U32)
    grid_spec = pltpu.PrefetchScalarGridSpec(
        num_scalar_prefetch=1,
        grid=(n // G,),
        in_specs=[
            pl.BlockSpec((G, w), lambda i, dest: (i, 0)),
            pl.BlockSpec(memory_space=pl.ANY),
        ],
        out_specs=pl.BlockSpec(memory_space=pl.ANY),
        scratch_shapes=[pltpu.SemaphoreType.DMA],
    )
    return pl.pallas_call(
        functools.partial(_dispatch_kernel, G=G),
        out_shape=jax.ShapeDtypeStruct((P, w), U32),
        grid_spec=grid_spec,
        input_output_aliases={2: 0},
        compiler_params=_cparams(("arbitrary",)),
        name="moe_dispatch",
    )(dest, hp, xs_init)


def _moe_kernel(win_ref, e_ref, lo_ref, hi_ref, first_ref,
                x_ref, wg_ref, wl_ref, wd_ref, bg_ref, bl_ref, bd_ref, y_ref,
                xb_s, acc_s, wg_s, wl_s, wd_s, *, NJ):
    del win_ref, e_ref
    w = pl.program_id(0)
    j = pl.program_id(1)
    lo = lo_ref[w]
    hi = hi_ref[w]
    half = x_ref.shape[1]

    @pl.when(jnp.logical_and(j == 0, first_ref[w] == 1))
    def _():
        y_ref[...] = jnp.zeros(y_ref.shape, U32)

    @pl.when(hi > lo)
    def _():
        wg_s[...] = wg_ref[...].astype(BF16)
        wl_s[...] = wl_ref[...].astype(BF16)
        wd_s[...] = wd_ref[...].astype(BF16)

    def sub(s, carry):
        rows = pl.ds(pl.multiple_of(s * MOE_SUB, MOE_SUB), MOE_SUB)

        @pl.when(j == 0)
        def _():
            xw = x_ref[rows, :]
            xb_s[rows, 0:half] = _unpack_hi(xw).astype(BF16)
            xb_s[rows, half:2 * half] = _unpack_lo(xw).astype(BF16)

        xb = xb_s[rows, :]
        hg = jnp.minimum(_dot(xb, wg_s[...]) + bg_ref[...], SWIGLU_LIMIT)
        hl = jnp.clip(_dot(xb, wl_s[...]) + bl_ref[...], -SWIGLU_LIMIT, SWIGLU_LIMIT)
        act = (hg * _sigmoid(SWIGLU_ALPHA * hg) * (hl + 1.0)).astype(BF16)
        contrib = _dot(act, wd_s[...])

        @pl.when(j == 0)
        def _():
            acc_s[rows, :] = contrib

        @pl.when(j > 0)
        def _():
            acc_s[rows, :] += contrib

        @pl.when(j == NJ - 1)
        def _():
            yv = acc_s[rows, :] + bd_ref[...]
            y_ref[rows, :] = _pack_pair(yv[:, :half], yv[:, half:])

        return carry

    lax.fori_loop(lo, hi, sub, 0)


def _moe(plan, xs, w_gu, b_gu, w_dn, b_dn):
    P, half = xs.shape
    d = 2 * half
    E, _, f2 = w_gu.shape
    f = f2 // 2
    nj = f // MOE_TF
    ni = plan[0].shape[0]
    grid_spec = pltpu.PrefetchScalarGridSpec(
        num_scalar_prefetch=5,
        grid=(ni, nj),
        in_specs=[
            pl.BlockSpec((MOE_WIN, half), lambda w, j, win, e, lo, hi, fi: (win[w], 0)),
            pl.BlockSpec((None, d, MOE_TF), lambda w, j, win, e, lo, hi, fi: (e[w], 0, j)),
            pl.BlockSpec((None, d, MOE_TF), lambda w, j, win, e, lo, hi, fi: (e[w], 0, nj + j)),
            pl.BlockSpec((None, MOE_TF, d), lambda w, j, win, e, lo, hi, fi: (e[w], j, 0)),
            pl.BlockSpec((None, 1, MOE_TF), lambda w, j, win, e, lo, hi, fi: (e[w], 0, j)),
            pl.BlockSpec((None, 1, MOE_TF), lambda w, j, win, e, lo, hi, fi: (e[w], 0, nj + j)),
            pl.BlockSpec((None, 1, d), lambda w, j, win, e, lo, hi, fi: (e[w], 0, 0)),
        ],
        out_specs=pl.BlockSpec((MOE_WIN, half), lambda w, j, win, e, lo, hi, fi: (win[w], 0)),
        scratch_shapes=[
            pltpu.VMEM((MOE_WIN, d), BF16),
            pltpu.VMEM((MOE_WIN, d), F32),
            pltpu.VMEM((d, MOE_TF), BF16),
            pltpu.VMEM((d, MOE_TF), BF16),
            pltpu.VMEM((MOE_TF, d), BF16),
        ],
    )
    return pl.pallas_call(
        functools.partial(_moe_kernel, NJ=nj),
        out_shape=jax.ShapeDtypeStruct((P, half), U32),
        grid_spec=grid_spec,
        compiler_params=_cparams(("arbitrary", "arbitrary")),
        name="moe_experts",
    )(*plan, xs, w_gu, w_gu, w_dn, b_gu, b_gu, b_dn)


def _final_kernel(dest_ref, x1_ref, gate_ref, p_ref, wple_ref, wpg_ref, gpost_ref, gple_ref, gfin_ref,
                  y_ref, o_ref, yg_s, sem, *, TM, final_norm):
    base = pl.program_id(0) * TM

    def issue(r, carry):
        for k in range(TOP_K):
            d = dest_ref[(base + r) * TOP_K + k]
            pltpu.make_async_copy(y_ref.at[pl.ds(d, 1)], yg_s.at[k, pl.ds(r, 1)], sem).start()
        return carry

    lax.fori_loop(0, TM, issue, 0)
    for k in range(TOP_K):
        pltpu.make_async_copy(y_ref.at[pl.ds(0, TM)], yg_s.at[k], sem).wait()

    x1 = x1_ref[...]
    half = x1.shape[1] // 2
    lo_sum = x1[:, half:]
    hi_sum = x1[:, :half]
    for k in range(TOP_K):
        gk = gate_ref[:, k:k + 1]
        yw = yg_s[k]
        hi_sum = hi_sum + gk * _unpack_hi(yw)
        lo_sum = lo_sum + gk * _unpack_lo(yw)
    x2 = jnp.concatenate([hi_sum, lo_sum], axis=1)
    emb = _rms(_dot(p_ref[...].astype(BF16), wple_ref[...]), gpost_ref[...])
    pg = _sigmoid(_dot(_rms(x2, gple_ref[...]).astype(BF16), wpg_ref[...]))
    x3 = x2 + pg * emb
    if final_norm:
        x3 = _rms(x3, gfin_ref[...])
    o_ref[...] = x3


def _final(dest, x1, gates, p2, w_ple, w_pg, g_post, g_ple, g_fin, y, tm, final_norm):
    n, d = x1.shape
    pd = p2.shape[1]
    const = lambda i, dest: (0, 0)
    grid_spec = pltpu.PrefetchScalarGridSpec(
        num_scalar_prefetch=1,
        grid=(n // tm,),
        in_specs=[
            pl.BlockSpec((tm, d), lambda i, dest: (i, 0)),
            pl.BlockSpec((tm, LANES), lambda i, dest: (i, 0)),
            pl.BlockSpec((tm, pd), lambda i, dest: (i, 0)),
            pl.BlockSpec((pd, d), const),
            pl.BlockSpec((d, d), const),
            pl.BlockSpec((1, d), const),
            pl.BlockSpec((1, d), const),
            pl.BlockSpec((1, d), const),
            pl.BlockSpec(memory_space=pl.ANY),
        ],
        out_specs=pl.BlockSpec((tm, d), lambda i, dest: (i, 0)),
        scratch_shapes=[pltpu.VMEM((TOP_K, tm, d // 2), U32), pltpu.SemaphoreType.DMA],
    )
    return pl.pallas_call(
        functools.partial(_final_kernel, TM=tm, final_norm=final_norm),
        out_shape=jax.ShapeDtypeStruct((n, d), F32),
        grid_spec=grid_spec,
        compiler_params=_cparams(("arbitrary",)),
        name="combine_ple",
    )(dest, x1, gates, p2, w_ple, w_pg, g_post, g_ple, g_fin, y)


def _layer(x2, p2, batch, seq, g_mix, w_in, conv_w, conv_b, b_if, g_mh, g_sgu, w_s, b_s, w_br, w_out,
           g_ffn, w_router, b_router, w_gu, b_gu, w_dn, b_dn, g_ple, w_pg, w_ple, g_ple_post, g_final,
           final_norm):
    n, d = x2.shape
    H = M_HEADS
    E = w_router.shape[1]
    row = lambda v: v.reshape(1, -1).astype(F32)

    c_if = 3 * d
    w_if = jnp.pad(w_in[:, c_if:c_if + 2 * H], ((0, 0), (0, LANES - 2 * H))).astype(BF16)
    bias_if = jnp.pad(b_if.reshape(1, 2 * H), ((0, 0), (0, LANES - 2 * H))).astype(F32)
    w_main = jnp.concatenate([w_in[:, :c_if], w_in[:, c_if + 2 * H:]], axis=1).astype(BF16)

    tm_a = min(512, n)
    xn, z_if = _input_norm(x2, row(g_mix), w_if, bias_if, tm_a)

    tm = min(1024, n)
    tn = 1024
    n_plain = 2 * d // tn
    sig_tiles = tuple(range(2 * d // tn, 3 * d // tn)) + tuple(range(5 * d // tn, 7 * d // tn))
    z_main = _input_proj(xn, w_main, tm, tn, n_plain, sig_tiles)

    zif_t = z_if[:, :2 * H].T
    y_a = _mlstm(z_main, zif_t, conv_w.astype(F32), row(conv_b), row(g_mh), batch, seq, d)
    y_b = _sgu(z_main, w_s.astype(F32), b_s.T.astype(F32), row(g_sgu), d, min(512, n))
    mix = _merge(y_a, y_b, z_main, w_br.astype(BF16), d, tm, 512)

    w_r = jnp.pad(w_router, ((0, 0), (0, LANES - E))).astype(F32)
    b_r = jnp.pad(b_router.reshape(1, E), ((0, 0), (0, LANES - E))).astype(F32)
    x1, hp, idx, gates = _out_proj(x2, mix, w_out.astype(BF16), row(g_ffn), w_r, b_r, E, min(256, n))

    P = n * TOP_K + E * MOE_SUB
    P = -(-P // MOE_WIN) * MOE_WIN
    dest, plan = _route_plan(idx[:, :TOP_K], E, P)
    xs = _dispatch(dest, hp, P, min(512, n))
    y = _moe(plan, xs, w_gu, b_gu.reshape(E, 1, -1), w_dn, b_dn.reshape(E, 1, -1))
    return _final(dest, x1, gates, p2, w_ple.astype(BF16), w_pg.astype(BF16), row(g_ple_post),
                  row(g_ple), row(g_final), y, min(256, n), final_norm)


def kernel(x, p, g_mix, w_in, conv_w, conv_b, b_if, g_mh, g_sgu, w_s, b_s, w_br, w_out, g_ffn, w_router,
           b_router, w_gu, b_gu, w_dn, b_dn, g_ple, w_pg, w_ple, g_ple_post, g_final):
    batch, seq, d = x.shape
    depth = p.shape[0]
    x2 = x.reshape(batch * seq, d)
    for i in range(depth):
        x2 = _layer(x2, p[i].reshape(batch * seq, -1), batch, seq, g_mix[i], w_in[i], conv_w[i], conv_b[i],
                    b_if[i], g_mh[i], g_sgu[i], w_s[i], b_s[i], w_br[i], w_out[i], g_ffn[i], w_router[i],
                    b_router[i], w_gu[i], b_gu[i], w_dn[i], b_dn[i], g_ple[i], w_pg[i], w_ple[i],
                    g_ple_post[i], g_final, final_norm=(i == depth - 1))
    return x2.reshape(batch, seq, d)
```

```python
import functools

import jax
import jax.numpy as jnp
from jax import lax
from jax.experimental import pallas as pl
from jax.experimental.pallas import tpu as pltpu

F32 = jnp.float32
BF16 = jnp.bfloat16
I32 = jnp.int32

RMS_EPS = 1e-6
M_HEADS = 8
CONV_W = 4
G_GROUPS = 8
G_CHUNK = 128
TOP_K = 4
SWIGLU_ALPHA = 1.702
SWIGLU_LIMIT = 7.0

LANES = 128
MLSTM_CHUNK = 256
MOE_SUB = 256
MOE_WIN = 1024
MOE_TF = 256
VMEM_LIMIT = 56 * 1024 * 1024


def _cparams(sem, vmem=VMEM_LIMIT):
    return pltpu.CompilerParams(dimension_semantics=sem, vmem_limit_bytes=vmem)


def _dot(a, b, **kw):
    return jnp.dot(a, b, preferred_element_type=F32, **kw)


def _rms(x, g):
    ms = jnp.mean(x * x, axis=-1, keepdims=True)
    return x * lax.rsqrt(ms + RMS_EPS) * g


def _sigmoid(x):
    return 0.5 * jnp.tanh(0.5 * x) + 0.5


def _norm_kernel(x_ref, g_ref, wif_ref, bif_ref, xn_ref, zif_ref):
    xn = _rms(x_ref[...], g_ref[...]).astype(BF16)
    xn_ref[...] = xn
    zif_ref[...] = _dot(xn, wif_ref[...]) + bif_ref[...]


def _input_norm(x2, g, w_if, b_if, tm):
    n, d = x2.shape
    return pl.pallas_call(
        _norm_kernel,
        out_shape=(jax.ShapeDtypeStruct((n, d), BF16), jax.ShapeDtypeStruct((n, LANES), F32)),
        grid=(n // tm,),
        in_specs=[
            pl.BlockSpec((tm, d), lambda i: (i, 0)),
            pl.BlockSpec((1, d), lambda i: (0, 0)),
            pl.BlockSpec((d, LANES), lambda i: (0, 0)),
            pl.BlockSpec((1, LANES), lambda i: (0, 0)),
        ],
        out_specs=(pl.BlockSpec((tm, d), lambda i: (i, 0)), pl.BlockSpec((tm, LANES), lambda i: (i, 0))),
        compiler_params=_cparams(("arbitrary",)),
        name="input_norm",
    )(x2, g, w_if, b_if)


def _inproj_kernel(x_ref, wa_ref, wb_ref, o_ref, w_s, *, n_plain, n_aligned, sig_tiles, shift, chunks):
    j = pl.program_id(0)

    @pl.when(pl.program_id(1) == 0)
    def _():
        @pl.when(j < n_aligned)
        def _():
            w_s[...] = wa_ref[...].astype(BF16)

        @pl.when(j >= n_aligned)
        def _():
            w_s[...] = jnp.concatenate([wa_ref[:, shift:], wb_ref[:, :shift]], axis=1).astype(BF16)

    cm = x_ref.shape[0] // chunks

    def body(epilogue):
        for r in range(chunks):
            rows = slice(r * cm, (r + 1) * cm)
            o_ref[rows, :] = epilogue(_dot(x_ref[rows, :], w_s[...])).astype(o_ref.dtype)

    is_sig = functools.reduce(jnp.logical_or, [j == t for t in sig_tiles])
    is_plain = j < n_plain
    pl.when(is_plain)(lambda: body(lambda a: a))
    pl.when(is_sig)(lambda: body(_sigmoid))
    pl.when(jnp.logical_not(jnp.logical_or(is_plain, is_sig)))(lambda: body(jax.nn.gelu))


def _input_proj(xn, w_in, c_out, tm, tn, n_plain, n_aligned, sig_tiles, shift):
    n, d = xn.shape
    return pl.pallas_call(
        functools.partial(_inproj_kernel, n_plain=n_plain, n_aligned=n_aligned, sig_tiles=sig_tiles,
                          shift=shift, chunks=max(1, tm // 256)),
        out_shape=jax.ShapeDtypeStruct((n, c_out), BF16),
        grid=(c_out // tn, n // tm),
        in_specs=[
            pl.BlockSpec((tm, d), lambda j, i: (i, 0)),
            pl.BlockSpec((d, tn), lambda j, i: (0, j)),
            pl.BlockSpec((d, tn), lambda j, i: (0, jnp.where(j >= n_aligned, j + 1, n_aligned + 1))),
        ],
        out_specs=pl.BlockSpec((tm, tn), lambda j, i: (i, j)),
        scratch_shapes=[pltpu.VMEM((d, tn), BF16)],
        compiler_params=_cparams(("arbitrary", "arbitrary")),
        name="input_proj",
    )(xn, w_in, w_in)


def _log_sigmoid(x):
    return jnp.minimum(x, 0.0) - jnp.log1p(jnp.exp(-jnp.abs(x)))


def _cumsum_lanes(x):
    n = x.shape[1]
    lane = lax.broadcasted_iota(I32, x.shape, 1)
    sh = 1
    while sh < n:
        x = x + jnp.where(lane >= sh, pltpu.roll(x, sh, axis=1), 0.0)
        sh *= 2
    return x


def _mlstm_kernel(qk_ref, v_ref, o_ref, zr_ref, cw_ref, cb_ref, gmh_ref, y_ref,
                  ext_s, qk_s, c_s, m_s, g_s, *, L, H, DK, DV):
    @pl.when(pl.program_id(1) == 0)
    def _():
        ext_s[0:8, :] = jnp.zeros((8, ext_s.shape[1]), F32)
        c_s[...] = jnp.zeros(c_s.shape, F32)
        m_s[...] = jnp.zeros(m_s.shape, F32)
        g_s[...] = jnp.zeros(g_s.shape, F32)

    u = qk_ref[...].astype(F32)
    ext_s[8:8 + L, :] = u
    cw = cw_ref[...]
    conv = (cb_ref[...] + cw[3:4] * u + cw[2:3] * ext_s[7:7 + L, :]
            + cw[1:2] * ext_s[6:6 + L, :] + cw[0:1] * ext_s[5:5 + L, :])
    ext_s[0:8, :] = ext_s[L:L + 8, :]
    qk_s[...] = conv * _sigmoid(conv)

    zr = zr_ref[...]
    i_rows = zr[0:H]
    b_rows = _cumsum_lanes(_log_sigmoid(zr[H:2 * H]))
    g_s[0:H, :] = b_rows
    b_cols = g_s[...].T

    row = lax.broadcasted_iota(I32, (L, L), 0)
    col = lax.broadcasted_iota(I32, (L, L), 1)
    causal = row >= col
    ones_blk = jnp.where(lax.broadcasted_iota(I32, (L, LANES), 1) == 0, 1.0, 0.0).astype(BF16)

    for h in range(H):
        q = qk_s[:, h * DK:(h + 1) * DK]
        k = qk_s[:, (H + h) * DK:(H + h + 1) * DK] * (DK ** -0.5)
        kt = k.T
        b_col = b_cols[:, h:h + 1]
        b_row = b_rows[h:h + 1, :]
        i_row = i_rows[h:h + 1, :]
        m_prev = m_s[h:h + 1, 0:1]
        dmat = jnp.where(causal, b_col - b_row + i_row, -jnp.inf)
        m_inter = b_col + m_prev
        m_j = jnp.maximum(m_inter, jnp.max(dmat, axis=1, keepdims=True))
        q16 = q.astype(BF16)
        s = _dot(q16, kt.astype(BF16)) * jnp.exp(dmat - m_j)
        inter = jnp.exp(m_inter - m_j)
        vx = jnp.concatenate([v_ref[:, h * DV:(h + 1) * DV], ones_blk], axis=1)
        cx = c_s[h]
        nd = _dot(s.astype(BF16), vx) + inter * _dot(q16, cx.astype(BF16))
        den = nd[:, DV:DV + 1]
        hh = nd[:, 0:DV] / jnp.maximum(jnp.abs(den), jnp.exp(-m_j))
        b_last = b_row[:, L - 1:L]
        w_log = b_last - b_row + i_row
        m_new = jnp.maximum(b_last + m_prev, jnp.max(w_log, axis=1, keepdims=True))
        ktw = (kt * jnp.exp(w_log - m_new)).astype(BF16)
        c_s[h] = jnp.exp(b_last + m_prev - m_new) * cx + _dot(ktw, vx)
        m_s[h:h + 1, :] = jnp.broadcast_to(m_new, (1, LANES))
        yn = _rms(hh, gmh_ref[:, h * DV:(h + 1) * DV])
        y_ref[:, h * DV:(h + 1) * DV] = (yn * o_ref[:, h * DV:(h + 1) * DV].astype(F32)).astype(y_ref.dtype)


def _mlstm(z_main, zif_t, conv_w, conv_b, g_mh, batch, seq, d):
    L, H = MLSTM_CHUNK, M_HEADS
    DV = d // H
    DK = DV // 2
    nc = seq // L
    n = batch * seq
    row_blk = lambda b, c: b * nc + c
    return pl.pallas_call(
        functools.partial(_mlstm_kernel, L=L, H=H, DK=DK, DV=DV),
        out_shape=jax.ShapeDtypeStruct((n, d), BF16),
        grid=(batch, nc),
        in_specs=[
            pl.BlockSpec((L, d), lambda b, c: (row_blk(b, c), 0)),
            pl.BlockSpec((L, d), lambda b, c: (row_blk(b, c), 1)),
            pl.BlockSpec((L, d), lambda b, c: (row_blk(b, c), 2)),
            pl.BlockSpec((2 * H, L), lambda b, c: (0, row_blk(b, c))),
            pl.BlockSpec((CONV_W, d), lambda b, c: (0, 0)),
            pl.BlockSpec((1, d), lambda b, c: (0, 0)),
            pl.BlockSpec((1, d), lambda b, c: (0, 0)),
        ],
        out_specs=pl.BlockSpec((L, d), lambda b, c: (row_blk(b, c), 0)),
        scratch_shapes=[
            pltpu.VMEM((L + 8, d), F32),
            pltpu.VMEM((L, d), F32),
            pltpu.VMEM((H, DK, DV + LANES), F32),
            pltpu.VMEM((H, LANES), F32),
            pltpu.VMEM((LANES, L), F32),
        ],
        compiler_params=_cparams(("arbitrary", "arbitrary")),
        name="mlstm",
    )(z_main, z_main, z_main, zif_t, conv_w, conv_b, g_mh)


def _sgu_kernel(u_ref, v_ref, ws_ref, bst_ref, g_ref, y_ref, vn_s, *, R, DG):
    vn_s[...] = _rms(v_ref[...].astype(F32), g_ref[...]).astype(BF16)
    row = lax.broadcasted_iota(I32, (G_CHUNK, G_CHUNK), 0)
    col = lax.broadcasted_iota(I32, (G_CHUNK, G_CHUNK), 1)
    for g in range(G_GROUPS):
        ws = jnp.where(row >= col, ws_ref[g], 0.0).astype(BF16)
        bias = bst_ref[:, g:g + 1]
        for c in range(R // G_CHUNK):
            rs = slice(c * G_CHUNK, (c + 1) * G_CHUNK)
            cs = slice(g * DG, (g + 1) * DG)
            vm = _dot(ws, vn_s[rs, cs]) + bias
            y_ref[rs, cs] = (u_ref[rs, cs].astype(F32) * vm).astype(y_ref.dtype)


def _sgu(z_main, w_s, b_s_t, g_sgu, d, R):
    n = z_main.shape[0]
    return pl.pallas_call(
        functools.partial(_sgu_kernel, R=R, DG=d // G_GROUPS),
        out_shape=jax.ShapeDtypeStruct((n, d), BF16),
        grid=(n // R,),
        in_specs=[
            pl.BlockSpec((R, d), lambda i: (i, 3)),
            pl.BlockSpec((R, d), lambda i: (i, 4)),
            pl.BlockSpec((G_GROUPS, G_CHUNK, G_CHUNK), lambda i: (0, 0, 0)),
            pl.BlockSpec((G_CHUNK, G_GROUPS), lambda i: (0, 0)),
            pl.BlockSpec((1, d), lambda i: (0, 0)),
        ],
        out_specs=pl.BlockSpec((R, d), lambda i: (i, 0)),
        scratch_shapes=[pltpu.VMEM((R, d), BF16)],
        compiler_params=_cparams(("arbitrary",)),
        name="spatial_gating",
    )(z_main, z_main, w_s, b_s_t, g_sgu)


def _merge_kernel(ya_ref, yb_ref, ga_ref, gb_ref, w0_ref, w1_ref, o_ref):
    a = _dot(ya_ref[...], w0_ref[...])
    b = _dot(yb_ref[...], w1_ref[...])
    o_ref[...] = (ga_ref[...].astype(F32) * a + gb_ref[...].astype(F32) * b).astype(o_ref.dtype)


def _merge(y_a, y_b, z_main, w_br, d, tm, tn):
    n = y_a.shape[0]
    ga0 = 5 * d // tn
    gb0 = 6 * d // tn
    return pl.pallas_call(
        _merge_kernel,
        out_shape=jax.ShapeDtypeStruct((n, d), BF16),
        grid=(d // tn, n // tm),
        in_specs=[
            pl.BlockSpec((tm, d), lambda j, i: (i, 0)),
            pl.BlockSpec((tm, d), lambda j, i: (i, 0)),
            pl.BlockSpec((tm, tn), lambda j, i: (i, ga0 + j)),
            pl.BlockSpec((tm, tn), lambda j, i: (i, gb0 + j)),
            pl.BlockSpec((None, d, tn), lambda j, i: (0, 0, j)),
            pl.BlockSpec((None, d, tn), lambda j, i: (1, 0, j)),
        ],
        out_specs=pl.BlockSpec((tm, tn), lambda j, i: (i, j)),
        compiler_params=_cparams(("arbitrary", "arbitrary")),
        name="branch_merge",
    )(y_a, y_b, z_main, z_main, w_br, w_br)


def _split_bf16(x):
    hi = x.astype(BF16)
    return hi, (x - hi.astype(F32)).astype(BF16)


def _outproj_kernel(x_ref, mix_ref, wo_ref, gf_ref, wr_ref, br_ref,
                    x1_ref, hp_ref, idx_ref, gate_ref, cnt_ref, cnt_s, *, E):
    @pl.when(pl.program_id(0) == 0)
    def _():
        cnt_s[...] = jnp.zeros(cnt_s.shape, F32)

    x1 = x_ref[...] + _dot(mix_ref[...], wo_ref[...])
    x1_ref[...] = x1
    hn = _rms(x1, gf_ref[...])
    hp_ref[...] = hn
    h_hi, h_lo = _split_bf16(hn)
    w_hi, w_lo = _split_bf16(wr_ref[...])
    logits = _dot(h_hi, w_hi) + (_dot(h_hi, w_lo) + _dot(h_lo, w_hi)) + br_ref[...]
    lane = lax.broadcasted_iota(I32, logits.shape, 1)
    work = jnp.where(lane < E, logits, -jnp.inf)
    vals, idxs = [], []
    for _ in range(TOP_K):
        mx = jnp.max(work, axis=1, keepdims=True)
        ix = jnp.min(jnp.where(work == mx, lane, LANES), axis=1, keepdims=True)
        vals.append(mx)
        idxs.append(ix)
        work = jnp.where(lane == ix, -jnp.inf, work)
    exps = [jnp.exp(v - vals[0]) for v in vals]
    inv = 1.0 / functools.reduce(jnp.add, exps)
    tm = logits.shape[0]
    onehots = [jnp.where(lane == ix, 1.0, 0.0) for ix in idxs]
    oh_sum = functools.reduce(jnp.add, onehots)
    earlier = lax.broadcasted_iota(I32, (tm, tm), 0) > lax.broadcasted_iota(I32, (tm, tm), 1)
    before = _dot(jnp.where(earlier, 1.0, 0.0).astype(BF16), oh_sum.astype(BF16)) + cnt_s[0:1, :]
    cnt = cnt_s[0:1, :] + jnp.sum(oh_sum, axis=0, keepdims=True)
    cnt_s[...] = jnp.broadcast_to(cnt, cnt_s.shape)
    cnt_ref[...] = jnp.broadcast_to(cnt, cnt_s.shape).astype(I32)
    idx_out = jnp.zeros(logits.shape, I32)
    gate_out = jnp.zeros(logits.shape, F32)
    for k in range(TOP_K):
        rank = jnp.sum(onehots[k] * before, axis=1, keepdims=True).astype(I32)
        idx_out = jnp.where(lane == k, idxs[k], idx_out)
        idx_out = jnp.where(lane == TOP_K + k, rank, idx_out)
        gate_out = jnp.where(lane == k, exps[k] * inv, gate_out)
    idx_ref[...] = idx_out
    gate_ref[...] = gate_out


def _out_proj(x2, mix, w_out, g_ffn, w_router, b_router, E, tm):
    n, d = x2.shape
    return pl.pallas_call(
        functools.partial(_outproj_kernel, E=E),
        out_shape=(
            jax.ShapeDtypeStruct((n, d), F32),
            jax.ShapeDtypeStruct((n, d), F32),
            jax.ShapeDtypeStruct((n, LANES), I32),
            jax.ShapeDtypeStruct((n, LANES), F32),
            jax.ShapeDtypeStruct((8, LANES), I32),
        ),
        grid=(n // tm,),
        in_specs=[
            pl.BlockSpec((tm, d), lambda i: (i, 0)),
            pl.BlockSpec((tm, d), lambda i: (i, 0)),
            pl.BlockSpec((d, d), lambda i: (0, 0)),
            pl.BlockSpec((1, d), lambda i: (0, 0)),
            pl.BlockSpec((d, LANES), lambda i: (0, 0)),
            pl.BlockSpec((1, LANES), lambda i: (0, 0)),
        ],
        out_specs=(
            pl.BlockSpec((tm, d), lambda i: (i, 0)),
            pl.BlockSpec((tm, d), lambda i: (i, 0)),
            pl.BlockSpec((tm, LANES), lambda i: (i, 0)),
            pl.BlockSpec((tm, LANES), lambda i: (i, 0)),
            pl.BlockSpec((8, LANES), lambda i: (0, 0)),
        ),
        scratch_shapes=[pltpu.VMEM((8, LANES), F32)],
        compiler_params=_cparams(("arbitrary",)),
        name="out_proj_router",
    )(x2, mix, w_out, g_ffn, w_router, b_router)


def _route_plan(top_idx, rank, counts, E, P):
    n, k = top_idx.shape
    a = n * k
    e_flat = top_idx.reshape(a)
    padded = ((counts + MOE_SUB - 1) // MOE_SUB) * MOE_SUB
    pend = jnp.cumsum(padded)
    pstart = pend - padded
    onehot = e_flat[:, None] == jnp.arange(E, dtype=I32)[None, :]
    dest = (rank.reshape(a) + jnp.sum(jnp.where(onehot, pstart[None, :], 0), axis=1)).astype(I32)

    nsb = P // MOE_SUB
    spw = MOE_WIN // MOE_SUB
    ni = P // MOE_WIN + E
    sb = jnp.arange(nsb, dtype=I32)
    sb_e = jnp.sum((pend[None, :] <= (sb * MOE_SUB)[:, None]).astype(I32), axis=1)
    sb_e = jnp.minimum(sb_e, E - 1)
    valid = sb * MOE_SUB < pend[-1]
    sb_e = jnp.where(valid, sb_e, sb_e[pend[-1] // MOE_SUB - 1])
    prev_e = jnp.concatenate([jnp.full((1,), -1, I32), sb_e[:-1]])
    is_start = (sb % spw == 0) | (sb_e != prev_e)
    item_of_sb = jnp.cumsum(is_start.astype(I32)) - 1
    n_items = jnp.sum(is_start.astype(I32))
    slot = jnp.where(is_start, item_of_sb, ni)
    item_win = jnp.zeros((ni,), I32).at[slot].set(sb // spw, mode="drop")
    item_e = jnp.zeros((ni,), I32).at[slot].set(sb_e, mode="drop")
    item_lo = jnp.zeros((ni,), I32).at[slot].set(sb % spw, mode="drop")
    item_first = jnp.zeros((ni,), I32).at[slot].set((sb % spw == 0).astype(I32), mode="drop")
    item_len = jnp.zeros((ni,), I32).at[jnp.where(valid, item_of_sb, ni)].add(1, mode="drop")
    it = jnp.arange(ni, dtype=I32)
    live = it < n_items
    last = jnp.maximum(n_items - 1, 0)
    src = jnp.minimum(it, last)
    item_win = item_win[src]
    item_e = item_e[src]
    item_lo = jnp.where(live, item_lo, 0)
    item_hi = jnp.where(live, item_lo + item_len, 0)
    item_first = jnp.where(live, item_first, 0)
    padinfo = jnp.concatenate([pstart + counts, padded - counts, pend[-1:] // MOE_SUB]).astype(I32)
    return dest, padinfo, (item_win, item_e, item_lo, item_hi, item_first)


def _dispatch_kernel(dest_ref, pad_ref, hp_ref, xs_ref, sem, pad_sem, tail_sem, *, G, E):
    base = pl.program_id(0) * G

    def row_copy(r, d, s):
        return pltpu.make_async_copy(hp_ref.at[pl.ds(r, 1)], xs_ref.at[pl.ds(d, 1)], s)

    def issue(r, carry):
        for k in range(TOP_K):
            row_copy(r, dest_ref[(base + r) * TOP_K + k], sem).start()
        return carry

    lax.fori_loop(0, G, issue, 0)

    @pl.when(pl.program_id(0) == 0)
    def _():
        def fill(e, carry):
            start = pad_ref[e]

            def one(r, c):
                row_copy(0, start + r, pad_sem).start()
                return c

            return lax.fori_loop(0, pad_ref[E + e], one, carry)

        def drain(e, carry):
            def one(r, c):
                row_copy(0, 0, pad_sem).wait()
                return c

            return lax.fori_loop(0, pad_ref[E + e], one, carry)

        def tail_copy(s):
            rows = pl.ds(pl.multiple_of(s * MOE_SUB, MOE_SUB), MOE_SUB)
            return pltpu.make_async_copy(hp_ref.at[pl.ds(0, MOE_SUB)], xs_ref.at[rows], tail_sem)

        def tail_fill(s, carry):
            tail_copy(s).start()
            return carry

        def tail_drain(s, carry):
            tail_copy(s).wait()
            return carry

        n_sub = xs_ref.shape[0] // MOE_SUB
        lax.fori_loop(0, E, fill, 0)
        lax.fori_loop(pad_ref[2 * E], n_sub, tail_fill, 0)
        lax.fori_loop(0, E, drain, 0)
        lax.fori_loop(pad_ref[2 * E], n_sub, tail_drain, 0)

    pltpu.make_async_copy(xs_ref.at[pl.ds(0, G * TOP_K)], xs_ref.at[pl.ds(0, G * TOP_K)], sem).wait()


def _dispatch(dest, padinfo, hp, P, G, E):
    n, w = hp.shape
    grid_spec = pltpu.PrefetchScalarGridSpec(
        num_scalar_prefetch=2,
        grid=(n // G,),
        in_specs=[pl.BlockSpec((G, w), lambda i, dest, pad: (i, 0))],
        out_specs=pl.BlockSpec(memory_space=pl.ANY),
        scratch_shapes=[pltpu.SemaphoreType.DMA] * 3,
    )
    return pl.pallas_call(
        functools.partial(_dispatch_kernel, G=G, E=E),
        out_shape=jax.ShapeDtypeStruct((P, w), F32),
        grid_spec=grid_spec,
        compiler_params=_cparams(("arbitrary",)),
        name="moe_dispatch",
    )(dest, padinfo, hp)


def _moe_kernel(win_ref, e_ref, lo_ref, hi_ref, first_ref,
                xs_ref, wg_ref, wl_ref, wd_ref, bg_ref, bl_ref, bd_ref, y_ref,
                stage_s, xb_s, wg_s, wl_s, wd_s, sem, *, NI, NJ):
    del e_ref
    w = pl.program_id(0)
    j = pl.program_id(1)
    lo = lo_ref[w]
    hi = hi_ref[w]
    spw = MOE_WIN // MOE_SUB
    d = y_ref.shape[1]

    def sub_rows(s):
        return pl.ds(pl.multiple_of(s * MOE_SUB, MOE_SUB), MOE_SUB)

    def x_copy(item, s):
        src = pl.ds(pl.multiple_of((win_ref[item] * spw + s) * MOE_SUB, MOE_SUB), MOE_SUB)
        return pltpu.make_async_copy(xs_ref.at[src], stage_s.at[sub_rows(s)], sem)

    def start_item(item):
        def body(s, carry):
            x_copy(item, s).start()
            return carry

        lax.fori_loop(lo_ref[item], hi_ref[item], body, 0)

    @pl.when(j == 0)
    def _():
        @pl.when(w == 0)
        def _():
            start_item(0)

        @pl.when(first_ref[w] == 1)
        def _():
            y_ref[...] = jnp.zeros(y_ref.shape, F32)

        def wait_one(s, carry):
            x_copy(w, s).wait()
            return carry

        def take(s, carry):
            rows = sub_rows(s)
            xb_s[rows, :] = stage_s[rows, :].astype(BF16)
            y_ref[rows, :] = jnp.broadcast_to(bd_ref[...], (MOE_SUB, d))
            return carry

        lax.fori_loop(lo, hi, wait_one, 0)
        lax.fori_loop(lo, hi, take, 0)

    @pl.when(jnp.logical_and(j == 1, w + 1 < NI))
    def _():
        start_item(w + 1)

    def mlp(r0, nrows):
        rows = pl.ds(r0, nrows)
        xb = xb_s[rows, :]
        hg = jnp.minimum(_dot(xb, wg_s[...]) + bg_ref[...], SWIGLU_LIMIT)
        hl = jnp.clip(_dot(xb, wl_s[...]) + bl_ref[...], -SWIGLU_LIMIT, SWIGLU_LIMIT)
        act = (hg * _sigmoid(SWIGLU_ALPHA * hg) * (hl + 1.0)).astype(BF16)
        y_ref[rows, :] += _dot(act, wd_s[...])

    for g in range(1, spw + 1):
        @pl.when(hi - lo == g)
        def _(g=g):
            wg_s[...] = wg_ref[...].astype(BF16)
            wl_s[...] = wl_ref[...].astype(BF16)
            wd_s[...] = wd_ref[...].astype(BF16)
            done = 0
            while done < g:
                step = min(2, g - done)
                mlp(pl.multiple_of((lo + done) * MOE_SUB, MOE_SUB), step * MOE_SUB)
                done += step


def _moe(plan, xs, w_gu, b_gu, w_dn, b_dn):
    P, d = xs.shape
    E, _, f2 = w_gu.shape
    f = f2 // 2
    nj = f // MOE_TF
    assert nj >= 2, "the next item's rows are prefetched at the second hidden tile"
    ni = plan[0].shape[0]
    grid_spec = pltpu.PrefetchScalarGridSpec(
        num_scalar_prefetch=5,
        grid=(ni, nj),
        in_specs=[
            pl.BlockSpec(memory_space=pl.ANY),
            pl.BlockSpec((None, d, MOE_TF), lambda w, j, win, e, lo, hi, fi: (e[w], 0, j)),
            pl.BlockSpec((None, d, MOE_TF), lambda w, j, win, e, lo, hi, fi: (e[w], 0, nj + j)),
            pl.BlockSpec((None, MOE_TF, d), lambda w, j, win, e, lo, hi, fi: (e[w], j, 0)),
            pl.BlockSpec((None, 1, MOE_TF), lambda w, j, win, e, lo, hi, fi: (e[w], 0, j)),
            pl.BlockSpec((None, 1, MOE_TF), lambda w, j, win, e, lo, hi, fi: (e[w], 0, nj + j)),
            pl.BlockSpec((None, 1, d), lambda w, j, win, e, lo, hi, fi: (e[w], 0, 0)),
        ],
        out_specs=pl.BlockSpec((MOE_WIN, d), lambda w, j, win, e, lo, hi, fi: (win[w], 0)),
        scratch_shapes=[
            pltpu.VMEM((MOE_WIN, d), F32),
            pltpu.VMEM((MOE_WIN, d), BF16),
            pltpu.VMEM((d, MOE_TF), BF16),
            pltpu.VMEM((d, MOE_TF), BF16),
            pltpu.VMEM((MOE_TF, d), BF16),
            pltpu.SemaphoreType.DMA,
        ],
    )
    return pl.pallas_call(
        functools.partial(_moe_kernel, NI=ni, NJ=nj),
        out_shape=jax.ShapeDtypeStruct((P, d), F32),
        grid_spec=grid_spec,
        compiler_params=_cparams(("arbitrary", "arbitrary")),
        name="moe_experts",
    )(*plan, xs, w_gu, w_gu, w_dn, b_gu, b_gu, b_dn)


def _final_kernel(dest_ref, x1_ref, gate_ref, p_ref, wple_ref, wpg_ref, gpost_ref, gple_ref, gfin_ref,
                  y_ref, o_ref, yg_s, sem, *, TM, NT, final_norm):
    i = pl.program_id(0)

    def gather(tile, slot):
        def issue(r, carry):
            for k in range(TOP_K):
                d = dest_ref[(tile * TM + r) * TOP_K + k]
                pltpu.make_async_copy(y_ref.at[pl.ds(d, 1)], yg_s.at[slot, k, pl.ds(r, 1)],
                                      sem.at[slot]).start()
            return carry

        lax.fori_loop(0, TM, issue, 0)

    @pl.when(i == 0)
    def _():
        gather(0, 0)

    @pl.when(i + 1 < NT)
    def _():
        gather(i + 1, (i + 1) % 2)

    slot = i % 2
    for k in range(TOP_K):
        pltpu.make_async_copy(y_ref.at[pl.ds(0, TM)], yg_s.at[slot, k], sem.at[slot]).wait()

    x2 = x1_ref[...]
    for k in range(TOP_K):
        x2 = x2 + gate_ref[:, k:k + 1] * yg_s[slot, k]
    emb = _rms(_dot(p_ref[...].astype(BF16), wple_ref[...]), gpost_ref[...])
    pg = _sigmoid(_dot(_rms(x2, gple_ref[...]).astype(BF16), wpg_ref[...]))
    x3 = x2 + pg * emb
    if final_norm:
        x3 = _rms(x3, gfin_ref[...])
    o_ref[...] = x3


def _final(dest, x1, gates, p2, w_ple, w_pg, g_post, g_ple, g_fin, y, tm, final_norm):
    n, d = x1.shape
    pd = p2.shape[1]
    const = lambda i, dest: (0, 0)
    grid_spec = pltpu.PrefetchScalarGridSpec(
        num_scalar_prefetch=1,
        grid=(n // tm,),
        in_specs=[
            pl.BlockSpec((tm, d), lambda i, dest: (i, 0)),
            pl.BlockSpec((tm, LANES), lambda i, dest: (i, 0)),
            pl.BlockSpec((tm, pd), lambda i, dest: (i, 0)),
            pl.BlockSpec((pd, d), const),
            pl.BlockSpec((d, d), const),
            pl.BlockSpec((1, d), const),
            pl.BlockSpec((1, d), const),
            pl.BlockSpec((1, d), const),
            pl.BlockSpec(memory_space=pl.ANY),
        ],
        out_specs=pl.BlockSpec((tm, d), lambda i, dest: (i, 0)),
        scratch_shapes=[pltpu.VMEM((2, TOP_K, tm, d), F32), pltpu.SemaphoreType.DMA((2,))],
    )
    return pl.pallas_call(
        functools.partial(_final_kernel, TM=tm, NT=n // tm, final_norm=final_norm),
        out_shape=jax.ShapeDtypeStruct((n, d), F32),
        grid_spec=grid_spec,
        compiler_params=_cparams(("arbitrary",)),
        name="combine_ple",
    )(dest, x1, gates, p2, w_ple, w_pg, g_post, g_ple, g_fin, y)


def _layer(x2, p2, batch, seq, g_mix, w_in, conv_w, conv_b, b_if, g_mh, g_sgu, w_s, b_s, w_br, w_out,
           g_ffn, w_router, b_router, w_gu, b_gu, w_dn, b_dn, g_ple, w_pg, w_ple, g_ple_post, g_final,
           final_norm):
    n, d = x2.shape
    H = M_HEADS
    E = w_router.shape[1]
    row = lambda v: v.reshape(1, -1).astype(F32)

    c_if = 3 * d
    w_if = jnp.pad(w_in[:, c_if:c_if + 2 * H], ((0, 0), (0, LANES - 2 * H))).astype(BF16)
    bias_if = jnp.pad(b_if.reshape(1, 2 * H), ((0, 0), (0, LANES - 2 * H))).astype(F32)

    tm_a = min(512, n)
    xn, z_if = _input_norm(x2, row(g_mix), w_if, bias_if, tm_a)

    tm = min(1024, n)
    tn = 1024
    n_plain = 2 * d // tn
    sig_tiles = tuple(range(2 * d // tn, 3 * d // tn)) + tuple(range(5 * d // tn, 7 * d // tn))
    z_main = _input_proj(xn, w_in, 7 * d, tm, tn, n_plain, c_if // tn, sig_tiles, 2 * H)

    zif_t = z_if[:, :2 * H].T
    y_a = _mlstm(z_main, zif_t, conv_w.astype(F32), row(conv_b), row(g_mh), batch, seq, d)
    y_b = _sgu(z_main, w_s.astype(F32), b_s.T.astype(F32), row(g_sgu), d, min(512, n))
    mix = _merge(y_a, y_b, z_main, w_br.astype(BF16), d, tm, 512)

    w_r = jnp.pad(w_router, ((0, 0), (0, LANES - E))).astype(F32)
    b_r = jnp.pad(b_router.reshape(1, E), ((0, 0), (0, LANES - E))).astype(F32)
    x1, hp, idx, gates, cnt = _out_proj(x2, mix, w_out.astype(BF16), row(g_ffn), w_r, b_r, E, min(256, n))

    P = n * TOP_K + E * MOE_SUB
    P = -(-P // MOE_WIN) * MOE_WIN
    dest, padinfo, plan = _route_plan(idx[:, :TOP_K], idx[:, TOP_K:2 * TOP_K], cnt[0, :E], E, P)
    xs = _dispatch(dest, padinfo, hp, P, min(512, n), E)
    y = _moe(plan, xs, w_gu, b_gu.reshape(E, 1, -1), w_dn, b_dn.reshape(E, 1, -1))
    return _final(dest, x1, gates, p2, w_ple.astype(BF16), w_pg.astype(BF16), row(g_ple_post),
                  row(g_ple), row(g_final), y, min(256, n), final_norm)


def kernel(x, p, g_mix, w_in, conv_w, conv_b, b_if, g_mh, g_sgu, w_s, b_s, w_br, w_out, g_ffn, w_router,
           b_router, w_gu, b_gu, w_dn, b_dn, g_ple, w_pg, w_ple, g_ple_post, g_final):
    batch, seq, d = x.shape
    depth = p.shape[0]
    x2 = x.reshape(batch * seq, d)
    for i in range(depth):
        x2 = _layer(x2, p[i].reshape(batch * seq, -1), batch, seq, g_mix[i], w_in[i], conv_w[i], conv_b[i],
                    b_if[i], g_mh[i], g_sgu[i], w_s[i], b_s[i], w_br[i], w_out[i], g_ffn[i], w_router[i],
                    b_router[i], w_gu[i], b_gu[i], w_dn[i], b_dn[i], g_ple[i], w_pg[i], w_ple[i],
                    g_ple_post[i], g_final, final_norm=(i == depth - 1))
    return x2.reshape(batch, seq, d)
```

```python
import functools

import jax
import jax.numpy as jnp
from jax import lax
from jax.experimental import pallas as pl
from jax.experimental.pallas import tpu as pltpu

F32 = jnp.float32
BF16 = jnp.bfloat16
I32 = jnp.int32

RMS_EPS = 1e-6
M_HEADS = 8
CONV_W = 4
G_GROUPS = 8
G_CHUNK = 128
TOP_K = 4
SWIGLU_ALPHA = 1.702
SWIGLU_LIMIT = 7.0

LANES = 128
MLSTM_CHUNK = 256
MOE_SUB = 256
MOE_WIN = 1024
MOE_TF = 512
VMEM_LIMIT = 56 * 1024 * 1024


def _cparams(sem, vmem=VMEM_LIMIT):
    return pltpu.CompilerParams(dimension_semantics=sem, vmem_limit_bytes=vmem)


def _dot(a, b, **kw):
    return jnp.dot(a, b, preferred_element_type=F32, **kw)


def _rms(x, g):
    ms = jnp.mean(x * x, axis=-1, keepdims=True)
    return x * lax.rsqrt(ms + RMS_EPS) * g


def _sigmoid(x):
    return 0.5 * jnp.tanh(0.5 * x) + 0.5


def _norm_kernel(x_ref, g_ref, wif_ref, bif_ref, xn_ref, zif_ref):
    xn = _rms(x_ref[...], g_ref[...]).astype(BF16)
    xn_ref[...] = xn
    zif_ref[...] = _dot(xn, wif_ref[...]) + bif_ref[...]


def _input_norm(x2, g, w_if, b_if, tm):
    n, d = x2.shape
    return pl.pallas_call(
        _norm_kernel,
        out_shape=(jax.ShapeDtypeStruct((n, d), BF16), jax.ShapeDtypeStruct((n, LANES), F32)),
        grid=(n // tm,),
        in_specs=[
            pl.BlockSpec((tm, d), lambda i: (i, 0)),
            pl.BlockSpec((1, d), lambda i: (0, 0)),
            pl.BlockSpec((d, LANES), lambda i: (0, 0)),
            pl.BlockSpec((1, LANES), lambda i: (0, 0)),
        ],
        out_specs=(pl.BlockSpec((tm, d), lambda i: (i, 0)), pl.BlockSpec((tm, LANES), lambda i: (i, 0))),
        compiler_params=_cparams(("arbitrary",)),
        name="input_norm",
    )(x2, g, w_if, b_if)


def _inproj_kernel(x_ref, wa_ref, wb_ref, cin_ref, o_ref, cout_ref, w_s, *,
                   n_plain, n_aligned, sig_tiles, shift, chunks):
    j = pl.program_id(0)

    @pl.when(pl.program_id(1) == 0)
    def _():
        @pl.when(j < n_aligned)
        def _():
            w_s[...] = wa_ref[...].T.astype(BF16)

        @pl.when(j >= n_aligned)
        def _():
            w_s[...] = jnp.concatenate([wa_ref[shift:, :], wb_ref[...]], axis=0).T.astype(BF16)

    cm = x_ref.shape[0] // chunks

    def body(epilogue):
        cout_ref[...] = cin_ref[...].astype(BF16)
        for r in range(chunks):
            rows = slice(r * cm, (r + 1) * cm)
            o_ref[rows, :] = epilogue(_dot(x_ref[rows, :], w_s[...])).astype(o_ref.dtype)

    is_sig = functools.reduce(jnp.logical_or, [j == t for t in sig_tiles])
    is_plain = j < n_plain
    pl.when(is_plain)(lambda: body(lambda a: a))
    pl.when(is_sig)(lambda: body(_sigmoid))
    pl.when(jnp.logical_not(jnp.logical_or(is_plain, is_sig)))(lambda: body(jax.nn.gelu))


def _cast_blocks(rows, steps):
    cb = -(-rows // steps)
    cb = -(-cb // 16) * 16
    return cb, -(-rows // cb)


def _input_proj(xn, w_in_t, cast_src, c_out, tm, tn, n_plain, n_aligned, sig_tiles, shift):
    n, d = xn.shape
    ni = n // tm
    cast_rows, cast_cols = cast_src.shape
    cb, ncb = _cast_blocks(cast_rows, (c_out // tn) * ni)
    cast_map = lambda j, i: (jnp.minimum(j * ni + i, ncb - 1), 0)
    b_per_tile = tn // shift
    return pl.pallas_call(
        functools.partial(_inproj_kernel, n_plain=n_plain, n_aligned=n_aligned, sig_tiles=sig_tiles,
                          shift=shift, chunks=max(1, tm // 256)),
        out_shape=(jax.ShapeDtypeStruct((n, c_out), BF16), jax.ShapeDtypeStruct(cast_src.shape, BF16)),
        grid=(c_out // tn, ni),
        in_specs=[
            pl.BlockSpec((tm, d), lambda j, i: (i, 0)),
            pl.BlockSpec((tn, d), lambda j, i: (j, 0)),
            pl.BlockSpec((shift, d), lambda j, i: (jnp.maximum(j, n_aligned) * b_per_tile + b_per_tile, 0)),
            pl.BlockSpec((cb, cast_cols), cast_map),
        ],
        out_specs=(pl.BlockSpec((tm, tn), lambda j, i: (i, j)), pl.BlockSpec((cb, cast_cols), cast_map)),
        scratch_shapes=[pltpu.VMEM((d, tn), BF16)],
        compiler_params=_cparams(("arbitrary", "arbitrary")),
        name="input_proj",
    )(xn, w_in_t, w_in_t, cast_src)


def _log_sigmoid(x):
    return jnp.minimum(x, 0.0) - jnp.log1p(jnp.exp(-jnp.abs(x)))


def _cumsum_lanes(x):
    n = x.shape[1]
    lane = lax.broadcasted_iota(I32, x.shape, 1)
    sh = 1
    while sh < n:
        x = x + jnp.where(lane >= sh, pltpu.roll(x, sh, axis=1), 0.0)
        sh *= 2
    return x


def _mlstm_kernel(qk_ref, v_ref, o_ref, zr_ref, cw_ref, cb_ref, gmh_ref, cin_ref, y_ref, cout_ref,
                  ext_s, qk_s, c_s, m_s, g_s, *, L, H, DK, DV):
    cout_ref[...] = cin_ref[...].astype(BF16)

    @pl.when(pl.program_id(1) == 0)
    def _():
        ext_s[0:8, :] = jnp.zeros((8, ext_s.shape[1]), F32)
        c_s[...] = jnp.zeros(c_s.shape, F32)
        m_s[...] = jnp.zeros(m_s.shape, F32)
        g_s[...] = jnp.zeros(g_s.shape, F32)

    u = qk_ref[...].astype(F32)
    ext_s[8:8 + L, :] = u
    cw = cw_ref[...]
    conv = (cb_ref[...] + cw[3:4] * u + cw[2:3] * ext_s[7:7 + L, :]
            + cw[1:2] * ext_s[6:6 + L, :] + cw[0:1] * ext_s[5:5 + L, :])
    ext_s[0:8, :] = ext_s[L:L + 8, :]
    qk_s[...] = conv * _sigmoid(conv)

    zr = zr_ref[...]
    i_rows = zr[0:H]
    b_rows = _cumsum_lanes(_log_sigmoid(zr[H:2 * H]))
    g_s[0:H, :] = b_rows
    b_cols = g_s[...].T

    row = lax.broadcasted_iota(I32, (L, L), 0)
    col = lax.broadcasted_iota(I32, (L, L), 1)
    causal = row >= col
    ones_blk = jnp.where(lax.broadcasted_iota(I32, (L, LANES), 1) == 0, 1.0, 0.0).astype(BF16)

    for h in range(H):
        q = qk_s[:, h * DK:(h + 1) * DK]
        k = qk_s[:, (H + h) * DK:(H + h + 1) * DK] * (DK ** -0.5)
        kt = k.T
        b_col = b_cols[:, h:h + 1]
        b_row = b_rows[h:h + 1, :]
        i_row = i_rows[h:h + 1, :]
        m_prev = m_s[h:h + 1, 0:1]
        dmat = jnp.where(causal, b_col - b_row + i_row, -jnp.inf)
        m_inter = b_col + m_prev
        m_j = jnp.maximum(m_inter, jnp.max(dmat, axis=1, keepdims=True))
        q16 = q.astype(BF16)
        s = _dot(q16, kt.astype(BF16)) * jnp.exp(dmat - m_j)
        inter = jnp.exp(m_inter - m_j)
        vx = jnp.concatenate([v_ref[:, h * DV:(h + 1) * DV], ones_blk], axis=1)
        cx = c_s[h]
        nd = _dot(s.astype(BF16), vx) + inter * _dot(q16, cx.astype(BF16))
        den = nd[:, DV:DV + 1]
        hh = nd[:, 0:DV] / jnp.maximum(jnp.abs(den), jnp.exp(-m_j))
        b_last = b_row[:, L - 1:L]
        w_log = b_last - b_row + i_row
        m_new = jnp.maximum(b_last + m_prev, jnp.max(w_log, axis=1, keepdims=True))
        ktw = (kt * jnp.exp(w_log - m_new)).astype(BF16)
        c_s[h] = jnp.exp(b_last + m_prev - m_new) * cx + _dot(ktw, vx)
        m_s[h:h + 1, :] = jnp.broadcast_to(m_new, (1, LANES))
        yn = _rms(hh, gmh_ref[:, h * DV:(h + 1) * DV])
        y_ref[:, h * DV:(h + 1) * DV] = (yn * o_ref[:, h * DV:(h + 1) * DV].astype(F32)).astype(y_ref.dtype)


def _mlstm(z_main, zif_t, conv_w, conv_b, g_mh, cast_src, batch, seq, d):
    L, H = MLSTM_CHUNK, M_HEADS
    DV = d // H
    DK = DV // 2
    nc = seq // L
    n = batch * seq
    row_blk = lambda b, c: b * nc + c
    cast_rows, cast_cols = cast_src.shape
    cb, ncb = _cast_blocks(cast_rows, batch * nc)
    cast_map = lambda b, c: (jnp.minimum(row_blk(b, c), ncb - 1), 0)
    return pl.pallas_call(
        functools.partial(_mlstm_kernel, L=L, H=H, DK=DK, DV=DV),
        out_shape=(jax.ShapeDtypeStruct((n, d), BF16), jax.ShapeDtypeStruct(cast_src.shape, BF16)),
        grid=(batch, nc),
        in_specs=[
            pl.BlockSpec((L, d), lambda b, c: (row_blk(b, c), 0)),
            pl.BlockSpec((L, d), lambda b, c: (row_blk(b, c), 1)),
            pl.BlockSpec((L, d), lambda b, c: (row_blk(b, c), 2)),
            pl.BlockSpec((2 * H, L), lambda b, c: (0, row_blk(b, c))),
            pl.BlockSpec((CONV_W, d), lambda b, c: (0, 0)),
            pl.BlockSpec((1, d), lambda b, c: (0, 0)),
            pl.BlockSpec((1, d), lambda b, c: (0, 0)),
            pl.BlockSpec((cb, cast_cols), cast_map),
        ],
        out_specs=(pl.BlockSpec((L, d), lambda b, c: (row_blk(b, c), 0)),
                   pl.BlockSpec((cb, cast_cols), cast_map)),
        scratch_shapes=[
            pltpu.VMEM((L + 8, d), F32),
            pltpu.VMEM((L, d), F32),
            pltpu.VMEM((H, DK, DV + LANES), F32),
            pltpu.VMEM((H, LANES), F32),
            pltpu.VMEM((LANES, L), F32),
        ],
        compiler_params=_cparams(("arbitrary", "arbitrary")),
        name="mlstm",
    )(z_main, z_main, z_main, zif_t, conv_w, conv_b, g_mh, cast_src)


def _sgu_kernel(u_ref, v_ref, ws_ref, bst_ref, g_ref, y_ref, vn_s, *, R, DG):
    vn_s[...] = _rms(v_ref[...].astype(F32), g_ref[...]).astype(BF16)
    row = lax.broadcasted_iota(I32, (G_CHUNK, G_CHUNK), 0)
    col = lax.broadcasted_iota(I32, (G_CHUNK, G_CHUNK), 1)
    for g in range(G_GROUPS):
        ws = jnp.where(row >= col, ws_ref[g], 0.0).astype(BF16)
        bias = bst_ref[:, g:g + 1]
        for c in range(R // G_CHUNK):
            rs = slice(c * G_CHUNK, (c + 1) * G_CHUNK)
            cs = slice(g * DG, (g + 1) * DG)
            vm = _dot(ws, vn_s[rs, cs]) + bias
            y_ref[rs, cs] = (u_ref[rs, cs].astype(F32) * vm).astype(y_ref.dtype)


def _sgu(z_main, w_s, b_s_t, g_sgu, d, R):
    n = z_main.shape[0]
    return pl.pallas_call(
        functools.partial(_sgu_kernel, R=R, DG=d // G_GROUPS),
        out_shape=jax.ShapeDtypeStruct((n, d), BF16),
        grid=(n // R,),
        in_specs=[
            pl.BlockSpec((R, d), lambda i: (i, 3)),
            pl.BlockSpec((R, d), lambda i: (i, 4)),
            pl.BlockSpec((G_GROUPS, G_CHUNK, G_CHUNK), lambda i: (0, 0, 0)),
            pl.BlockSpec((G_CHUNK, G_GROUPS), lambda i: (0, 0)),
            pl.BlockSpec((1, d), lambda i: (0, 0)),
        ],
        out_specs=pl.BlockSpec((R, d), lambda i: (i, 0)),
        scratch_shapes=[pltpu.VMEM((R, d), BF16)],
        compiler_params=_cparams(("arbitrary",)),
        name="spatial_gating",
    )(z_main, z_main, w_s, b_s_t, g_sgu)


def _merge_kernel(ya_ref, yb_ref, ga_ref, gb_ref, w0_ref, w1_ref, o_ref):
    a = _dot(ya_ref[...], w0_ref[...])
    b = _dot(yb_ref[...], w1_ref[...])
    o_ref[...] = (ga_ref[...].astype(F32) * a + gb_ref[...].astype(F32) * b).astype(o_ref.dtype)


def _merge(y_a, y_b, z_main, w_br, d, tm, tn):
    n = y_a.shape[0]
    ga0 = 5 * d // tn
    gb0 = 6 * d // tn
    return pl.pallas_call(
        _merge_kernel,
        out_shape=jax.ShapeDtypeStruct((n, d), BF16),
        grid=(d // tn, n // tm),
        in_specs=[
            pl.BlockSpec((tm, d), lambda j, i: (i, 0)),
            pl.BlockSpec((tm, d), lambda j, i: (i, 0)),
            pl.BlockSpec((tm, tn), lambda j, i: (i, ga0 + j)),
            pl.BlockSpec((tm, tn), lambda j, i: (i, gb0 + j)),
            pl.BlockSpec((None, d, tn), lambda j, i: (0, 0, j)),
            pl.BlockSpec((None, d, tn), lambda j, i: (1, 0, j)),
        ],
        out_specs=pl.BlockSpec((tm, tn), lambda j, i: (i, j)),
        compiler_params=_cparams(("arbitrary", "arbitrary")),
        name="branch_merge",
    )(y_a, y_b, z_main, z_main, w_br, w_br)


def _split_bf16(x):
    hi = x.astype(BF16)
    return hi, (x - hi.astype(F32)).astype(BF16)


def _outproj_kernel(x_ref, mix_ref, wo_ref, gf_ref, wr_ref, br_ref,
                    x1_ref, hp_ref, idx_ref, gate_ref, cnt_ref, cnt_s, *, E):
    @pl.when(pl.program_id(0) == 0)
    def _():
        cnt_s[...] = jnp.zeros(cnt_s.shape, F32)

    x1 = x_ref[...] + _dot(mix_ref[...], wo_ref[...])
    x1_ref[...] = x1
    hn = _rms(x1, gf_ref[...])
    hp_ref[...] = hn
    h_hi, h_lo = _split_bf16(hn)
    w_hi, w_lo = _split_bf16(wr_ref[...])
    logits = _dot(h_hi, w_hi) + (_dot(h_hi, w_lo) + _dot(h_lo, w_hi)) + br_ref[...]
    lane = lax.broadcasted_iota(I32, logits.shape, 1)
    work = jnp.where(lane < E, logits, -jnp.inf)
    vals, idxs = [], []
    for _ in range(TOP_K):
        mx = jnp.max(work, axis=1, keepdims=True)
        ix = jnp.min(jnp.where(work == mx, lane, LANES), axis=1, keepdims=True)
        vals.append(mx)
        idxs.append(ix)
        work = jnp.where(lane == ix, -jnp.inf, work)
    exps = [jnp.exp(v - vals[0]) for v in vals]
    inv = 1.0 / functools.reduce(jnp.add, exps)
    tm = logits.shape[0]
    onehots = [jnp.where(lane == ix, 1.0, 0.0) for ix in idxs]
    oh_sum = functools.reduce(jnp.add, onehots)
    earlier = lax.broadcasted_iota(I32, (tm, tm), 0) > lax.broadcasted_iota(I32, (tm, tm), 1)
    before = _dot(jnp.where(earlier, 1.0, 0.0).astype(BF16), oh_sum.astype(BF16)) + cnt_s[0:1, :]
    cnt = cnt_s[0:1, :] + jnp.sum(oh_sum, axis=0, keepdims=True)
    cnt_s[...] = jnp.broadcast_to(cnt, cnt_s.shape)
    cnt_ref[...] = jnp.broadcast_to(cnt, cnt_s.shape).astype(I32)
    idx_out = jnp.zeros(logits.shape, I32)
    gate_out = jnp.zeros(logits.shape, F32)
    for k in range(TOP_K):
        rank = jnp.sum(onehots[k] * before, axis=1, keepdims=True).astype(I32)
        idx_out = jnp.where(lane == k, idxs[k], idx_out)
        idx_out = jnp.where(lane == TOP_K + k, rank, idx_out)
        gate_out = jnp.where(lane == k, exps[k] * inv, gate_out)
    idx_ref[...] = idx_out
    gate_ref[...] = gate_out


def _out_proj(x2, mix, w_out, g_ffn, w_router, b_router, E, tm):
    n, d = x2.shape
    return pl.pallas_call(
        functools.partial(_outproj_kernel, E=E),
        out_shape=(
            jax.ShapeDtypeStruct((n, d), F32),
            jax.ShapeDtypeStruct((n, d), F32),
            jax.ShapeDtypeStruct((n, LANES), I32),
            jax.ShapeDtypeStruct((n, LANES), F32),
            jax.ShapeDtypeStruct((8, LANES), I32),
        ),
        grid=(n // tm,),
        in_specs=[
            pl.BlockSpec((tm, d), lambda i: (i, 0)),
            pl.BlockSpec((tm, d), lambda i: (i, 0)),
            pl.BlockSpec((d, d), lambda i: (0, 0)),
            pl.BlockSpec((1, d), lambda i: (0, 0)),
            pl.BlockSpec((d, LANES), lambda i: (0, 0)),
            pl.BlockSpec((1, LANES), lambda i: (0, 0)),
        ],
        out_specs=(
            pl.BlockSpec((tm, d), lambda i: (i, 0)),
            pl.BlockSpec((tm, d), lambda i: (i, 0)),
            pl.BlockSpec((tm, LANES), lambda i: (i, 0)),
            pl.BlockSpec((tm, LANES), lambda i: (i, 0)),
            pl.BlockSpec((8, LANES), lambda i: (0, 0)),
        ),
        scratch_shapes=[pltpu.VMEM((8, LANES), F32)],
        compiler_params=_cparams(("arbitrary",)),
        name="out_proj_router",
    )(x2, mix, w_out, g_ffn, w_router, b_router)


def _route_plan(top_idx, rank, counts, E, P):
    n, k = top_idx.shape
    a = n * k
    e_flat = top_idx.reshape(a)
    padded = ((counts + MOE_SUB - 1) // MOE_SUB) * MOE_SUB
    pend = jnp.cumsum(padded)
    pstart = pend - padded
    onehot = e_flat[:, None] == jnp.arange(E, dtype=I32)[None, :]
    dest = (rank.reshape(a) + jnp.sum(jnp.where(onehot, pstart[None, :], 0), axis=1)).astype(I32)

    nsb = P // MOE_SUB
    spw = MOE_WIN // MOE_SUB
    ni = P // MOE_WIN + E
    sb = jnp.arange(nsb, dtype=I32)
    sb_e = jnp.sum((pend[None, :] <= (sb * MOE_SUB)[:, None]).astype(I32), axis=1)
    sb_e = jnp.minimum(sb_e, E - 1)
    valid = sb * MOE_SUB < pend[-1]
    sb_e = jnp.where(valid, sb_e, sb_e[pend[-1] // MOE_SUB - 1])
    prev_e = jnp.concatenate([jnp.full((1,), -1, I32), sb_e[:-1]])
    is_start = (sb % spw == 0) | (sb_e != prev_e)
    item_of_sb = jnp.cumsum(is_start.astype(I32)) - 1
    n_items = jnp.sum(is_start.astype(I32))
    slot = jnp.where(is_start, item_of_sb, ni)
    item_win = jnp.zeros((ni,), I32).at[slot].set(sb // spw, mode="drop")
    item_e = jnp.zeros((ni,), I32).at[slot].set(sb_e, mode="drop")
    item_lo = jnp.zeros((ni,), I32).at[slot].set(sb % spw, mode="drop")
    item_first = jnp.zeros((ni,), I32).at[slot].set((sb % spw == 0).astype(I32), mode="drop")
    item_len = jnp.zeros((ni,), I32).at[jnp.where(valid, item_of_sb, ni)].add(1, mode="drop")
    it = jnp.arange(ni, dtype=I32)
    live = it < n_items
    last = jnp.maximum(n_items - 1, 0)
    src = jnp.minimum(it, last)
    item_win = item_win[src]
    item_e = item_e[src]
    item_lo = jnp.where(live, item_lo, 0)
    item_hi = jnp.where(live, item_lo + item_len, 0)
    item_first = jnp.where(live, item_first, 0)
    padinfo = jnp.concatenate([pstart + counts, padded - counts, pend[-1:] // MOE_SUB]).astype(I32)
    return dest, padinfo, (item_win, item_e, item_lo, item_hi, item_first)


def _dispatch_kernel(dest_ref, pad_ref, hp_ref, xs_ref, sem, pad_sem, tail_sem, *, G, E):
    base = pl.program_id(0) * G

    def row_copy(r, d, s):
        return pltpu.make_async_copy(hp_ref.at[pl.ds(r, 1)], xs_ref.at[pl.ds(d, 1)], s)

    def issue(r, carry):
        for k in range(TOP_K):
            row_copy(r, dest_ref[(base + r) * TOP_K + k], sem).start()
        return carry

    lax.fori_loop(0, G, issue, 0)

    @pl.when(pl.program_id(0) == 0)
    def _():
        def fill(e, carry):
            start = pad_ref[e]

            def one(r, c):
                row_copy(0, start + r, pad_sem).start()
                return c

            return lax.fori_loop(0, pad_ref[E + e], one, carry)

        def drain(e, carry):
            def one(r, c):
                row_copy(0, 0, pad_sem).wait()
                return c

            return lax.fori_loop(0, pad_ref[E + e], one, carry)

        def tail_copy(s):
            rows = pl.ds(pl.multiple_of(s * MOE_SUB, MOE_SUB), MOE_SUB)
            return pltpu.make_async_copy(hp_ref.at[pl.ds(0, MOE_SUB)], xs_ref.at[rows], tail_sem)

        def tail_fill(s, carry):
            tail_copy(s).start()
            return carry

        def tail_drain(s, carry):
            tail_copy(s).wait()
            return carry

        n_sub = xs_ref.shape[0] // MOE_SUB
        lax.fori_loop(0, E, fill, 0)
        lax.fori_loop(pad_ref[2 * E], n_sub, tail_fill, 0)
        lax.fori_loop(0, E, drain, 0)
        lax.fori_loop(pad_ref[2 * E], n_sub, tail_drain, 0)

    pltpu.make_async_copy(xs_ref.at[pl.ds(0, G * TOP_K)], xs_ref.at[pl.ds(0, G * TOP_K)], sem).wait()


def _dispatch(dest, padinfo, hp, P, G, E):
    n, w = hp.shape
    grid_spec = pltpu.PrefetchScalarGridSpec(
        num_scalar_prefetch=2,
        grid=(n // G,),
        in_specs=[pl.BlockSpec((G, w), lambda i, dest, pad: (i, 0))],
        out_specs=pl.BlockSpec(memory_space=pl.ANY),
        scratch_shapes=[pltpu.SemaphoreType.DMA] * 3,
    )
    return pl.pallas_call(
        functools.partial(_dispatch_kernel, G=G, E=E),
        out_shape=jax.ShapeDtypeStruct((P, w), F32),
        grid_spec=grid_spec,
        compiler_params=_cparams(("arbitrary",)),
        name="moe_dispatch",
    )(dest, padinfo, hp)


def _moe_kernel(win_ref, e_ref, lo_ref, hi_ref, first_ref,
                xs_ref, wg_ref, wl_ref, wd_ref, bg_ref, bl_ref, bd_ref, y_ref,
                stage_s, xb_s, sem, *, NI, NJ):
    del e_ref
    w = pl.program_id(0)
    j = pl.program_id(1)
    lo = lo_ref[w]
    hi = hi_ref[w]
    spw = MOE_WIN // MOE_SUB
    d = y_ref.shape[1]

    def sub_rows(s):
        return pl.ds(pl.multiple_of(s * MOE_SUB, MOE_SUB), MOE_SUB)

    def x_copy(item, s):
        src = pl.ds(pl.multiple_of((win_ref[item] * spw + s) * MOE_SUB, MOE_SUB), MOE_SUB)
        return pltpu.make_async_copy(xs_ref.at[src], stage_s.at[sub_rows(s)], sem)

    def start_item(item):
        def body(s, carry):
            x_copy(item, s).start()
            return carry

        lax.fori_loop(lo_ref[item], hi_ref[item], body, 0)

    @pl.when(j == 0)
    def _():
        @pl.when(w == 0)
        def _():
            start_item(0)

        @pl.when(first_ref[w] == 1)
        def _():
            y_ref[...] = jnp.zeros(y_ref.shape, F32)

        def wait_one(s, carry):
            x_copy(w, s).wait()
            return carry

        def take(s, carry):
            rows = sub_rows(s)
            xb_s[rows, :] = stage_s[rows, :].astype(BF16)
            y_ref[rows, :] = jnp.broadcast_to(bd_ref[...], (MOE_SUB, d))
            return carry

        lax.fori_loop(lo, hi, wait_one, 0)
        lax.fori_loop(lo, hi, take, 0)

    @pl.when(jnp.logical_and(j == 1, w + 1 < NI))
    def _():
        start_item(w + 1)

    def mlp(r0, nrows):
        rows = pl.ds(r0, nrows)
        xb = xb_s[rows, :]
        hg = jnp.minimum(_dot(xb, wg_ref[...]) + bg_ref[...], SWIGLU_LIMIT)
        hl = jnp.clip(_dot(xb, wl_ref[...]) + bl_ref[...], -SWIGLU_LIMIT, SWIGLU_LIMIT)
        act = (hg * _sigmoid(SWIGLU_ALPHA * hg) * (hl + 1.0)).astype(BF16)
        y_ref[rows, :] += _dot(act, wd_ref[...])

    for g in range(1, spw + 1):
        @pl.when(hi - lo == g)
        def _(g=g):
            done = 0
            while done < g:
                step = min(2, g - done)
                mlp(pl.multiple_of((lo + done) * MOE_SUB, MOE_SUB), step * MOE_SUB)
                done += step


def _moe(plan, xs, w_gu, b_gu, w_dn, b_dn):
    P, d = xs.shape
    E, _, f2 = w_gu.shape
    f = f2 // 2
    nj = f // MOE_TF
    assert nj >= 2, "the next item's rows are prefetched at the second hidden tile"
    ni = plan[0].shape[0]
    grid_spec = pltpu.PrefetchScalarGridSpec(
        num_scalar_prefetch=5,
        grid=(ni, nj),
        in_specs=[
            pl.BlockSpec(memory_space=pl.ANY),
            pl.BlockSpec((None, d, MOE_TF), lambda w, j, win, e, lo, hi, fi: (e[w], 0, j)),
            pl.BlockSpec((None, d, MOE_TF), lambda w, j, win, e, lo, hi, fi: (e[w], 0, nj + j)),
            pl.BlockSpec((None, MOE_TF, d), lambda w, j, win, e, lo, hi, fi: (e[w], j, 0)),
            pl.BlockSpec((None, 1, MOE_TF), lambda w, j, win, e, lo, hi, fi: (e[w], 0, j)),
            pl.BlockSpec((None, 1, MOE_TF), lambda w, j, win, e, lo, hi, fi: (e[w], 0, nj + j)),
            pl.BlockSpec((None, 1, d), lambda w, j, win, e, lo, hi, fi: (e[w], 0, 0)),
        ],
        out_specs=pl.BlockSpec((MOE_WIN, d), lambda w, j, win, e, lo, hi, fi: (win[w], 0)),
        scratch_shapes=[
            pltpu.VMEM((MOE_WIN, d), F32),
            pltpu.VMEM((MOE_WIN, d), BF16),
            pltpu.SemaphoreType.DMA,
        ],
    )
    return pl.pallas_call(
        functools.partial(_moe_kernel, NI=ni, NJ=nj),
        out_shape=jax.ShapeDtypeStruct((P, d), F32),
        grid_spec=grid_spec,
        compiler_params=_cparams(("arbitrary", "arbitrary")),
        name="moe_experts",
    )(*plan, xs, w_gu, w_gu, w_dn, b_gu, b_gu, b_dn)


def _final_kernel(dest_ref, x1_ref, gate_ref, p_ref, wple_ref, wpg_ref, gpost_ref, gple_ref, gfin_ref,
                  y_ref, o_ref, yg_s, sem, *, TM, NT, final_norm):
    i = pl.program_id(0)

    def gather(tile, slot):
        def issue(r, carry):
            for k in range(TOP_K):
                d = dest_ref[(tile * TM + r) * TOP_K + k]
                pltpu.make_async_copy(y_ref.at[pl.ds(d, 1)], yg_s.at[slot, k, pl.ds(r, 1)],
                                      sem.at[slot]).start()
            return carry

        lax.fori_loop(0, TM, issue, 0)

    @pl.when(i == 0)
    def _():
        gather(0, 0)

    @pl.when(i + 1 < NT)
    def _():
        gather(i + 1, (i + 1) % 2)

    slot = i % 2
    for k in range(TOP_K):
        pltpu.make_async_copy(y_ref.at[pl.ds(0, TM)], yg_s.at[slot, k], sem.at[slot]).wait()

    x2 = x1_ref[...]
    for k in range(TOP_K):
        x2 = x2 + gate_ref[:, k:k + 1] * yg_s[slot, k]
    emb = _rms(_dot(p_ref[...].astype(BF16), wple_ref[...]), gpost_ref[...])
    pg = _sigmoid(_dot(_rms(x2, gple_ref[...]).astype(BF16), wpg_ref[...]))
    x3 = x2 + pg * emb
    if final_norm:
        x3 = _rms(x3, gfin_ref[...])
    o_ref[...] = x3


def _final(dest, x1, gates, p2, w_ple, w_pg, g_post, g_ple, g_fin, y, tm, final_norm):
    n, d = x1.shape
    pd = p2.shape[1]
    const = lambda i, dest: (0, 0)
    grid_spec = pltpu.PrefetchScalarGridSpec(
        num_scalar_prefetch=1,
        grid=(n // tm,),
        in_specs=[
            pl.BlockSpec((tm, d), lambda i, dest: (i, 0)),
            pl.BlockSpec((tm, LANES), lambda i, dest: (i, 0)),
            pl.BlockSpec((tm, pd), lambda i, dest: (i, 0)),
            pl.BlockSpec((pd, d), const),
            pl.BlockSpec((d, d), const),
            pl.BlockSpec((1, d), const),
            pl.BlockSpec((1, d), const),
            pl.BlockSpec((1, d), const),
            pl.BlockSpec(memory_space=pl.ANY),
        ],
        out_specs=pl.BlockSpec((tm, d), lambda i, dest: (i, 0)),
        scratch_shapes=[pltpu.VMEM((2, TOP_K, tm, d), F32), pltpu.SemaphoreType.DMA((2,))],
    )
    return pl.pallas_call(
        functools.partial(_final_kernel, TM=tm, NT=n // tm, final_norm=final_norm),
        out_shape=jax.ShapeDtypeStruct((n, d), F32),
        grid_spec=grid_spec,
        compiler_params=_cparams(("arbitrary",)),
        name="combine_ple",
    )(dest, x1, gates, p2, w_ple, w_pg, g_post, g_ple, g_fin, y)


def _layer(x2, p2, batch, seq, g_mix, w_in, conv_w, conv_b, b_if, g_mh, g_sgu, w_s, b_s, w_br, w_out,
           g_ffn, w_router, b_router, w_gu, b_gu, w_dn, b_dn, g_ple, w_pg, w_ple, g_ple_post, g_final,
           final_norm):
    n, d = x2.shape
    H = M_HEADS
    E = w_router.shape[1]
    row = lambda v: v.reshape(1, -1).astype(F32)

    c_if = 3 * d
    w_if = jnp.pad(w_in[:, c_if:c_if + 2 * H], ((0, 0), (0, LANES - 2 * H))).astype(BF16)
    bias_if = jnp.pad(b_if.reshape(1, 2 * H), ((0, 0), (0, LANES - 2 * H))).astype(F32)

    tm_a = min(512, n)
    xn, z_if = _input_norm(x2, row(g_mix), w_if, bias_if, tm_a)

    tm = min(1024, n)
    tn = 1024
    n_plain = 2 * d // tn
    sig_tiles = tuple(range(2 * d // tn, 3 * d // tn)) + tuple(range(5 * d // tn, 7 * d // tn))
    f2 = w_gu.shape[2]
    z_main, w_gu16 = _input_proj(xn, w_in.T, w_gu.reshape(E * d, f2), 7 * d, tm, tn, n_plain, c_if // tn,
                                 sig_tiles, 2 * H)

    zif_t = z_if[:, :2 * H].T
    y_a, w_dn16 = _mlstm(z_main, zif_t, conv_w.astype(F32), row(conv_b), row(g_mh),
                         w_dn.reshape(-1, d), batch, seq, d)
    y_b = _sgu(z_main, w_s.astype(F32), b_s.T.astype(F32), row(g_sgu), d, min(512, n))
    mix = _merge(y_a, y_b, z_main, w_br.astype(BF16), d, tm, 512)

    w_r = jnp.pad(w_router, ((0, 0), (0, LANES - E))).astype(F32)
    b_r = jnp.pad(b_router.reshape(1, E), ((0, 0), (0, LANES - E))).astype(F32)
    x1, hp, idx, gates, cnt = _out_proj(x2, mix, w_out.astype(BF16), row(g_ffn), w_r, b_r, E, min(256, n))

    P = n * TOP_K + E * MOE_SUB
    P = -(-P // MOE_WIN) * MOE_WIN
    dest, padinfo, plan = _route_plan(idx[:, :TOP_K], idx[:, TOP_K:2 * TOP_K], cnt[0, :E], E, P)
    xs = _dispatch(dest, padinfo, hp, P, min(512, n), E)
    y = _moe(plan, xs, w_gu16.reshape(w_gu.shape), b_gu.reshape(E, 1, -1), w_dn16.reshape(w_dn.shape),
             b_dn.reshape(E, 1, -1))
    return _final(dest, x1, gates, p2, w_ple.astype(BF16), w_pg.astype(BF16), row(g_ple_post),
                  row(g_ple), row(g_final), y, min(256, n), final_norm)


def kernel(x, p, g_mix, w_in, conv_w, conv_b, b_if, g_mh, g_sgu, w_s, b_s, w_br, w_out, g_ffn, w_router,
           b_router, w_gu, b_gu, w_dn, b_dn, g_ple, w_pg, w_ple, g_ple_post, g_final):
    batch, seq, d = x.shape
    depth = p.shape[0]
    x2 = x.reshape(batch * seq, d)
    for i in range(depth):
        x2 = _layer(x2, p[i].reshape(batch * seq, -1), batch, seq, g_mix[i], w_in[i], conv_w[i], conv_b[i],
                    b_if[i], g_mh[i], g_sgu[i], w_s[i], b_s[i], w_br[i], w_out[i], g_ffn[i], w_router[i],
                    b_router[i], w_gu[i], b_gu[i], w_dn[i], b_dn[i], g_ple[i], w_pg[i], w_ple[i],
                    g_ple_post[i], g_final, final_norm=(i == depth - 1))
    return x2.reshape(batch, seq, d)
```

```python
import functools

import jax
import jax.numpy as jnp
from jax import lax
from jax.experimental import pallas as pl
from jax.experimental.pallas import tpu as pltpu

F32 = jnp.float32
BF16 = jnp.bfloat16
I32 = jnp.int32

RMS_EPS = 1e-6
M_HEADS = 8
CONV_W = 4
G_GROUPS = 8
G_CHUNK = 128
TOP_K = 4
SWIGLU_ALPHA = 1.702
SWIGLU_LIMIT = 7.0

LANES = 128
MLSTM_CHUNK = 256
MOE_SUB = 256
MOE_WIN = 1024
MOE_TF = 512
VMEM_LIMIT = 56 * 1024 * 1024


def _cparams(sem, vmem=VMEM_LIMIT):
    return pltpu.CompilerParams(dimension_semantics=sem, vmem_limit_bytes=vmem)


def _dot(a, b, **kw):
    return jnp.dot(a, b, preferred_element_type=F32, **kw)


def _rms(x, g):
    ms = jnp.mean(x * x, axis=-1, keepdims=True)
    return x * lax.rsqrt(ms + RMS_EPS) * g


def _sigmoid(x):
    return 0.5 * jnp.tanh(0.5 * x) + 0.5


def _norm_kernel(x_ref, g_ref, wif_ref, bif_ref, xn_ref, zif_ref):
    xn = _rms(x_ref[...], g_ref[...]).astype(BF16)
    xn_ref[...] = xn
    zif_ref[...] = _dot(xn, wif_ref[...]) + bif_ref[...]


def _input_norm(x2, g, w_if, b_if, tm):
    n, d = x2.shape
    return pl.pallas_call(
        _norm_kernel,
        out_shape=(jax.ShapeDtypeStruct((n, d), BF16), jax.ShapeDtypeStruct((n, LANES), F32)),
        grid=(n // tm,),
        in_specs=[
            pl.BlockSpec((tm, d), lambda i: (i, 0)),
            pl.BlockSpec((1, d), lambda i: (0, 0)),
            pl.BlockSpec((d, LANES), lambda i: (0, 0)),
            pl.BlockSpec((1, LANES), lambda i: (0, 0)),
        ],
        out_specs=(pl.BlockSpec((tm, d), lambda i: (i, 0)), pl.BlockSpec((tm, LANES), lambda i: (i, 0))),
        compiler_params=_cparams(("arbitrary",)),
        name="input_norm",
    )(x2, g, w_if, b_if)


def _inproj_kernel(x_ref, wa_ref, wb_ref, cin_ref, o_ref, cout_ref, w_s, *,
                   n_plain, n_aligned, sig_tiles, shift, chunks):
    j = pl.program_id(0)

    @pl.when(pl.program_id(1) == 0)
    def _():
        @pl.when(j < n_aligned)
        def _():
            w_s[...] = wa_ref[...].T.astype(BF16)

        @pl.when(j >= n_aligned)
        def _():
            w_s[...] = jnp.concatenate([wa_ref[shift:, :], wb_ref[...]], axis=0).T.astype(BF16)

    cm = x_ref.shape[0] // chunks

    def body(epilogue):
        cout_ref[...] = cin_ref[...].astype(BF16)
        for r in range(chunks):
            rows = slice(r * cm, (r + 1) * cm)
            o_ref[rows, :] = epilogue(_dot(x_ref[rows, :], w_s[...])).astype(o_ref.dtype)

    is_sig = functools.reduce(jnp.logical_or, [j == t for t in sig_tiles])
    is_plain = j < n_plain
    pl.when(is_plain)(lambda: body(lambda a: a))
    pl.when(is_sig)(lambda: body(_sigmoid))
    pl.when(jnp.logical_not(jnp.logical_or(is_plain, is_sig)))(lambda: body(jax.nn.gelu))


def _cast_blocks(rows, steps):
    cb = -(-rows // steps)
    cb = -(-cb // 16) * 16
    return cb, -(-rows // cb)


def _input_proj(xn, w_in_t, cast_src, c_out, tm, tn, n_plain, n_aligned, sig_tiles, shift):
    n, d = xn.shape
    ni = n // tm
    cast_rows, cast_cols = cast_src.shape
    cb, ncb = _cast_blocks(cast_rows, (c_out // tn) * ni)
    cast_map = lambda j, i: (jnp.minimum(j * ni + i, ncb - 1), 0)
    b_per_tile = tn // shift
    return pl.pallas_call(
        functools.partial(_inproj_kernel, n_plain=n_plain, n_aligned=n_aligned, sig_tiles=sig_tiles,
                          shift=shift, chunks=max(1, tm // 256)),
        out_shape=(jax.ShapeDtypeStruct((n, c_out), BF16), jax.ShapeDtypeStruct(cast_src.shape, BF16)),
        grid=(c_out // tn, ni),
        in_specs=[
            pl.BlockSpec((tm, d), lambda j, i: (i, 0)),
            pl.BlockSpec((tn, d), lambda j, i: (j, 0)),
            pl.BlockSpec((shift, d), lambda j, i: (jnp.maximum(j, n_aligned) * b_per_tile + b_per_tile, 0)),
            pl.BlockSpec((cb, cast_cols), cast_map),
        ],
        out_specs=(pl.BlockSpec((tm, tn), lambda j, i: (i, j)), pl.BlockSpec((cb, cast_cols), cast_map)),
        scratch_shapes=[pltpu.VMEM((d, tn), BF16)],
        compiler_params=_cparams(("arbitrary", "arbitrary")),
        name="input_proj",
    )(xn, w_in_t, w_in_t, cast_src)


def _log_sigmoid(x):
    return jnp.minimum(x, 0.0) - jnp.log1p(jnp.exp(-jnp.abs(x)))


def _cumsum_lanes(x):
    n = x.shape[1]
    lane = lax.broadcasted_iota(I32, x.shape, 1)
    sh = 1
    while sh < n:
        x = x + jnp.where(lane >= sh, pltpu.roll(x, sh, axis=1), 0.0)
        sh *= 2
    return x


def _mlstm_kernel(qk_ref, v_ref, o_ref, zr_ref, cw_ref, cb_ref, gmh_ref, cin_ref, y_ref, cout_ref,
                  ext_s, qk_s, c_s, m_s, g_s, *, L, H, DK, DV):
    cout_ref[...] = cin_ref[...].astype(BF16)

    @pl.when(pl.program_id(1) == 0)
    def _():
        ext_s[0:8, :] = jnp.zeros((8, ext_s.shape[1]), F32)
        c_s[...] = jnp.zeros(c_s.shape, F32)
        m_s[...] = jnp.zeros(m_s.shape, F32)
        g_s[...] = jnp.zeros(g_s.shape, F32)

    u = qk_ref[...].astype(F32)
    ext_s[8:8 + L, :] = u
    cw = cw_ref[...]
    conv = (cb_ref[...] + cw[3:4] * u + cw[2:3] * ext_s[7:7 + L, :]
            + cw[1:2] * ext_s[6:6 + L, :] + cw[0:1] * ext_s[5:5 + L, :])
    ext_s[0:8, :] = ext_s[L:L + 8, :]
    qk_s[...] = conv * _sigmoid(conv)

    zr = zr_ref[...]
    i_rows = zr[0:H]
    b_rows = _cumsum_lanes(_log_sigmoid(zr[H:2 * H]))
    g_s[0:H, :] = b_rows
    b_cols = g_s[...].T

    row = lax.broadcasted_iota(I32, (L, L), 0)
    col = lax.broadcasted_iota(I32, (L, L), 1)
    causal = row >= col
    ones_blk = jnp.where(lax.broadcasted_iota(I32, (L, LANES), 1) == 0, 1.0, 0.0).astype(BF16)

    for h in range(H):
        q = qk_s[:, h * DK:(h + 1) * DK]
        k = qk_s[:, (H + h) * DK:(H + h + 1) * DK] * (DK ** -0.5)
        kt = k.T
        b_col = b_cols[:, h:h + 1]
        b_row = b_rows[h:h + 1, :]
        i_row = i_rows[h:h + 1, :]
        m_prev = m_s[h:h + 1, 0:1]
        dmat = jnp.where(causal, b_col - b_row + i_row, -jnp.inf)
        m_inter = b_col + m_prev
        m_j = jnp.maximum(m_inter, jnp.max(dmat, axis=1, keepdims=True))
        q16 = q.astype(BF16)
        s = _dot(q16, kt.astype(BF16)) * jnp.exp(dmat - m_j)
        inter = jnp.exp(m_inter - m_j)
        vx = jnp.concatenate([v_ref[:, h * DV:(h + 1) * DV], ones_blk], axis=1)
        cx = c_s[h]
        nd = _dot(s.astype(BF16), vx) + inter * _dot(q16, cx.astype(BF16))
        den = nd[:, DV:DV + 1]
        hh = nd[:, 0:DV] / jnp.maximum(jnp.abs(den), jnp.exp(-m_j))
        b_last = b_row[:, L - 1:L]
        w_log = b_last - b_row + i_row
        m_new = jnp.maximum(b_last + m_prev, jnp.max(w_log, axis=1, keepdims=True))
        ktw = (kt * jnp.exp(w_log - m_new)).astype(BF16)
        c_s[h] = jnp.exp(b_last + m_prev - m_new) * cx + _dot(ktw, vx)
        m_s[h:h + 1, :] = jnp.broadcast_to(m_new, (1, LANES))
        yn = _rms(hh, gmh_ref[:, h * DV:(h + 1) * DV])
        y_ref[:, h * DV:(h + 1) * DV] = (yn * o_ref[:, h * DV:(h + 1) * DV].astype(F32)).astype(y_ref.dtype)


def _mlstm(z_main, zif_t, conv_w, conv_b, g_mh, cast_src, batch, seq, d):
    L, H = MLSTM_CHUNK, M_HEADS
    DV = d // H
    DK = DV // 2
    nc = seq // L
    n = batch * seq
    row_blk = lambda b, c: b * nc + c
    cast_rows, cast_cols = cast_src.shape
    cb, ncb = _cast_blocks(cast_rows, batch * nc)
    cast_map = lambda b, c: (jnp.minimum(row_blk(b, c), ncb - 1), 0)
    return pl.pallas_call(
        functools.partial(_mlstm_kernel, L=L, H=H, DK=DK, DV=DV),
        out_shape=(jax.ShapeDtypeStruct((n, d), BF16), jax.ShapeDtypeStruct(cast_src.shape, BF16)),
        grid=(batch, nc),
        in_specs=[
            pl.BlockSpec((L, d), lambda b, c: (row_blk(b, c), 0)),
            pl.BlockSpec((L, d), lambda b, c: (row_blk(b, c), 1)),
            pl.BlockSpec((L, d), lambda b, c: (row_blk(b, c), 2)),
            pl.BlockSpec((2 * H, L), lambda b, c: (0, row_blk(b, c))),
            pl.BlockSpec((CONV_W, d), lambda b, c: (0, 0)),
            pl.BlockSpec((1, d), lambda b, c: (0, 0)),
            pl.BlockSpec((1, d), lambda b, c: (0, 0)),
            pl.BlockSpec((cb, cast_cols), cast_map),
        ],
        out_specs=(pl.BlockSpec((L, d), lambda b, c: (row_blk(b, c), 0)),
                   pl.BlockSpec((cb, cast_cols), cast_map)),
        scratch_shapes=[
            pltpu.VMEM((L + 8, d), F32),
            pltpu.VMEM((L, d), F32),
            pltpu.VMEM((H, DK, DV + LANES), F32),
            pltpu.VMEM((H, LANES), F32),
            pltpu.VMEM((LANES, L), F32),
        ],
        compiler_params=_cparams(("arbitrary", "arbitrary")),
        name="mlstm",
    )(z_main, z_main, z_main, zif_t, conv_w, conv_b, g_mh, cast_src)


def _sgu_kernel(u_ref, v_ref, ws_ref, bst_ref, g_ref, y_ref, vn_s, *, R, DG):
    vn_s[...] = _rms(v_ref[...].astype(F32), g_ref[...]).astype(BF16)
    row = lax.broadcasted_iota(I32, (G_CHUNK, G_CHUNK), 0)
    col = lax.broadcasted_iota(I32, (G_CHUNK, G_CHUNK), 1)
    for g in range(G_GROUPS):
        ws = jnp.where(row >= col, ws_ref[g], 0.0).astype(BF16)
        bias = bst_ref[:, g:g + 1]
        for c in range(R // G_CHUNK):
            rs = slice(c * G_CHUNK, (c + 1) * G_CHUNK)
            cs = slice(g * DG, (g + 1) * DG)
            vm = _dot(ws, vn_s[rs, cs]) + bias
            y_ref[rs, cs] = (u_ref[rs, cs].astype(F32) * vm).astype(y_ref.dtype)


def _sgu(z_main, w_s, b_s_t, g_sgu, d, R):
    n = z_main.shape[0]
    return pl.pallas_call(
        functools.partial(_sgu_kernel, R=R, DG=d // G_GROUPS),
        out_shape=jax.ShapeDtypeStruct((n, d), BF16),
        grid=(n // R,),
        in_specs=[
            pl.BlockSpec((R, d), lambda i: (i, 3)),
            pl.BlockSpec((R, d), lambda i: (i, 4)),
            pl.BlockSpec((G_GROUPS, G_CHUNK, G_CHUNK), lambda i: (0, 0, 0)),
            pl.BlockSpec((G_CHUNK, G_GROUPS), lambda i: (0, 0)),
            pl.BlockSpec((1, d), lambda i: (0, 0)),
        ],
        out_specs=pl.BlockSpec((R, d), lambda i: (i, 0)),
        scratch_shapes=[pltpu.VMEM((R, d), BF16)],
        compiler_params=_cparams(("arbitrary",)),
        name="spatial_gating",
    )(z_main, z_main, w_s, b_s_t, g_sgu)


def _merge_kernel(ya_ref, yb_ref, ga_ref, gb_ref, w0_ref, w1_ref, o_ref):
    a = _dot(ya_ref[...], w0_ref[...])
    b = _dot(yb_ref[...], w1_ref[...])
    o_ref[...] = (ga_ref[...].astype(F32) * a + gb_ref[...].astype(F32) * b).astype(o_ref.dtype)


def _merge(y_a, y_b, z_main, w_br, d, tm, tn):
    n = y_a.shape[0]
    ga0 = 5 * d // tn
    gb0 = 6 * d // tn
    return pl.pallas_call(
        _merge_kernel,
        out_shape=jax.ShapeDtypeStruct((n, d), BF16),
        grid=(d // tn, n // tm),
        in_specs=[
            pl.BlockSpec((tm, d), lambda j, i: (i, 0)),
            pl.BlockSpec((tm, d), lambda j, i: (i, 0)),
            pl.BlockSpec((tm, tn), lambda j, i: (i, ga0 + j)),
            pl.BlockSpec((tm, tn), lambda j, i: (i, gb0 + j)),
            pl.BlockSpec((None, d, tn), lambda j, i: (0, 0, j)),
            pl.BlockSpec((None, d, tn), lambda j, i: (1, 0, j)),
        ],
        out_specs=pl.BlockSpec((tm, tn), lambda j, i: (i, j)),
        compiler_params=_cparams(("arbitrary", "arbitrary")),
        name="branch_merge",
    )(y_a, y_b, z_main, z_main, w_br, w_br)


def _split_bf16(x):
    hi = x.astype(BF16)
    return hi, (x - hi.astype(F32)).astype(BF16)


def _outproj_kernel(x_ref, mix_ref, wo_ref, gf_ref, wr_ref, br_ref,
                    x1_ref, hp_ref, idx_ref, gate_ref, cnt_ref, cnt_s, *, E):
    @pl.when(pl.program_id(0) == 0)
    def _():
        cnt_s[...] = jnp.zeros(cnt_s.shape, F32)

    x1 = x_ref[...] + _dot(mix_ref[...], wo_ref[...])
    x1_ref[...] = x1
    hn = _rms(x1, gf_ref[...])
    hp_ref[...] = hn
    h_hi, h_lo = _split_bf16(hn)
    w_hi, w_lo = _split_bf16(wr_ref[...])
    logits = _dot(h_hi, w_hi) + (_dot(h_hi, w_lo) + _dot(h_lo, w_hi)) + br_ref[...]
    lane = lax.broadcasted_iota(I32, logits.shape, 1)
    work = jnp.where(lane < E, logits, -jnp.inf)
    vals, idxs = [], []
    for _ in range(TOP_K):
        mx = jnp.max(work, axis=1, keepdims=True)
        ix = jnp.min(jnp.where(work == mx, lane, LANES), axis=1, keepdims=True)
        vals.append(mx)
        idxs.append(ix)
        work = jnp.where(lane == ix, -jnp.inf, work)
    exps = [jnp.exp(v - vals[0]) for v in vals]
    inv = 1.0 / functools.reduce(jnp.add, exps)
    tm = logits.shape[0]
    onehots = [jnp.where(lane == ix, 1.0, 0.0) for ix in idxs]
    oh_sum = functools.reduce(jnp.add, onehots)
    earlier = lax.broadcasted_iota(I32, (tm, tm), 0) > lax.broadcasted_iota(I32, (tm, tm), 1)
    before = _dot(jnp.where(earlier, 1.0, 0.0).astype(BF16), oh_sum.astype(BF16)) + cnt_s[0:1, :]
    cnt = cnt_s[0:1, :] + jnp.sum(oh_sum, axis=0, keepdims=True)
    cnt_s[...] = jnp.broadcast_to(cnt, cnt_s.shape)
    cnt_ref[...] = jnp.broadcast_to(cnt, cnt_s.shape).astype(I32)
    idx_out = jnp.zeros(logits.shape, I32)
    gate_out = jnp.zeros(logits.shape, F32)
    for k in range(TOP_K):
        rank = jnp.sum(onehots[k] * before, axis=1, keepdims=True).astype(I32)
        idx_out = jnp.where(lane == k, idxs[k], idx_out)
        idx_out = jnp.where(lane == TOP_K + k, rank, idx_out)
        gate_out = jnp.where(lane == k, exps[k] * inv, gate_out)
    idx_ref[...] = idx_out
    gate_ref[...] = gate_out


def _out_proj(x2, mix, w_out, g_ffn, w_router, b_router, E, tm):
    n, d = x2.shape
    return pl.pallas_call(
        functools.partial(_outproj_kernel, E=E),
        out_shape=(
            jax.ShapeDtypeStruct((n, d), F32),
            jax.ShapeDtypeStruct((n, d), F32),
            jax.ShapeDtypeStruct((n, LANES), I32),
            jax.ShapeDtypeStruct((n, LANES), F32),
            jax.ShapeDtypeStruct((8, LANES), I32),
        ),
        grid=(n // tm,),
        in_specs=[
            pl.BlockSpec((tm, d), lambda i: (i, 0)),
            pl.BlockSpec((tm, d), lambda i: (i, 0)),
            pl.BlockSpec((d, d), lambda i: (0, 0)),
            pl.BlockSpec((1, d), lambda i: (0, 0)),
            pl.BlockSpec((d, LANES), lambda i: (0, 0)),
            pl.BlockSpec((1, LANES), lambda i: (0, 0)),
        ],
        out_specs=(
            pl.BlockSpec((tm, d), lambda i: (i, 0)),
            pl.BlockSpec((tm, d), lambda i: (i, 0)),
            pl.BlockSpec((tm, LANES), lambda i: (i, 0)),
            pl.BlockSpec((tm, LANES), lambda i: (i, 0)),
            pl.BlockSpec((8, LANES), lambda i: (0, 0)),
        ),
        scratch_shapes=[pltpu.VMEM((8, LANES), F32)],
        compiler_params=_cparams(("arbitrary",)),
        name="out_proj_router",
    )(x2, mix, w_out, g_ffn, w_router, b_router)


def _route_plan(top_idx, rank, counts, E, P):
    n, k = top_idx.shape
    a = n * k
    e_flat = top_idx.reshape(a)
    padded = ((counts + MOE_SUB - 1) // MOE_SUB) * MOE_SUB
    pend = jnp.cumsum(padded)
    pstart = pend - padded
    onehot = e_flat[:, None] == jnp.arange(E, dtype=I32)[None, :]
    dest = (rank.reshape(a) + jnp.sum(jnp.where(onehot, pstart[None, :], 0), axis=1)).astype(I32)

    nsb = P // MOE_SUB
    spw = MOE_WIN // MOE_SUB
    ni = P // MOE_WIN + E
    sb = jnp.arange(nsb, dtype=I32)
    sb_e = jnp.sum((pend[None, :] <= (sb * MOE_SUB)[:, None]).astype(I32), axis=1)
    sb_e = jnp.minimum(sb_e, E - 1)
    valid = sb * MOE_SUB < pend[-1]
    sb_e = jnp.where(valid, sb_e, sb_e[pend[-1] // MOE_SUB - 1])
    prev_e = jnp.concatenate([jnp.full((1,), -1, I32), sb_e[:-1]])
    is_start = (sb % spw == 0) | (sb_e != prev_e)
    item_of_sb = jnp.cumsum(is_start.astype(I32)) - 1
    n_items = jnp.sum(is_start.astype(I32))
    slot = jnp.where(is_start, item_of_sb, ni)
    item_win = jnp.zeros((ni,), I32).at[slot].set(sb // spw, mode="drop")
    item_e = jnp.zeros((ni,), I32).at[slot].set(sb_e, mode="drop")
    item_lo = jnp.zeros((ni,), I32).at[slot].set(sb % spw, mode="drop")
    item_first = jnp.zeros((ni,), I32).at[slot].set((sb % spw == 0).astype(I32), mode="drop")
    item_len = jnp.zeros((ni,), I32).at[jnp.where(valid, item_of_sb, ni)].add(1, mode="drop")
    it = jnp.arange(ni, dtype=I32)
    live = it < n_items
    last = jnp.maximum(n_items - 1, 0)
    src = jnp.minimum(it, last)
    item_win = item_win[src]
    item_e = item_e[src]
    item_lo = jnp.where(live, item_lo, 0)
    item_hi = jnp.where(live, item_lo + item_len, 0)
    item_first = jnp.where(live, item_first, 0)
    padinfo = jnp.concatenate([pstart + counts, padded - counts, pend[-1:] // MOE_SUB]).astype(I32)
    return dest, padinfo, (item_win, item_e, item_lo, item_hi, item_first)


def _dispatch_kernel(dest_ref, pad_ref, hp_ref, xs_ref, sem, pad_sem, tail_sem, *, G, E):
    base = pl.program_id(0) * G

    def row_copy(r, d, s):
        return pltpu.make_async_copy(hp_ref.at[pl.ds(r, 1)], xs_ref.at[pl.ds(d, 1)], s)

    def issue(r, carry):
        for k in range(TOP_K):
            row_copy(r, dest_ref[(base + r) * TOP_K + k], sem).start()
        return carry

    lax.fori_loop(0, G, issue, 0)

    @pl.when(pl.program_id(0) == 0)
    def _():
        def fill(e, carry):
            start = pad_ref[e]

            def one(r, c):
                row_copy(0, start + r, pad_sem).start()
                return c

            return lax.fori_loop(0, pad_ref[E + e], one, carry)

        def drain(e, carry):
            def one(r, c):
                row_copy(0, 0, pad_sem).wait()
                return c

            return lax.fori_loop(0, pad_ref[E + e], one, carry)

        def tail_copy(s):
            rows = pl.ds(pl.multiple_of(s * MOE_SUB, MOE_SUB), MOE_SUB)
            return pltpu.make_async_copy(hp_ref.at[pl.ds(0, MOE_SUB)], xs_ref.at[rows], tail_sem)

        def tail_fill(s, carry):
            tail_copy(s).start()
            return carry

        def tail_drain(s, carry):
            tail_copy(s).wait()
            return carry

        n_sub = xs_ref.shape[0] // MOE_SUB
        lax.fori_loop(0, E, fill, 0)
        lax.fori_loop(pad_ref[2 * E], n_sub, tail_fill, 0)
        lax.fori_loop(0, E, drain, 0)
        lax.fori_loop(pad_ref[2 * E], n_sub, tail_drain, 0)

    pltpu.make_async_copy(xs_ref.at[pl.ds(0, G * TOP_K)], xs_ref.at[pl.ds(0, G * TOP_K)], sem).wait()


def _dispatch(dest, padinfo, hp, P, G, E):
    n, w = hp.shape
    grid_spec = pltpu.PrefetchScalarGridSpec(
        num_scalar_prefetch=2,
        grid=(n // G,),
        in_specs=[pl.BlockSpec((G, w), lambda i, dest, pad: (i, 0))],
        out_specs=pl.BlockSpec(memory_space=pl.ANY),
        scratch_shapes=[pltpu.SemaphoreType.DMA] * 3,
    )
    return pl.pallas_call(
        functools.partial(_dispatch_kernel, G=G, E=E),
        out_shape=jax.ShapeDtypeStruct((P, w), F32),
        grid_spec=grid_spec,
        compiler_params=_cparams(("arbitrary",)),
        name="moe_dispatch",
    )(dest, padinfo, hp)


def _moe_kernel(win_ref, e_ref, lo_ref, hi_ref, first_ref,
                xs_ref, wg_ref, wl_ref, wd_ref, bg_ref, bl_ref, bd_ref, y_ref,
                stage_s, xb_s, sem, *, NI, NJ):
    del e_ref
    w = pl.program_id(0)
    j = pl.program_id(1)
    lo = lo_ref[w]
    hi = hi_ref[w]
    spw = MOE_WIN // MOE_SUB
    d = y_ref.shape[1]

    def sub_rows(s):
        return pl.ds(pl.multiple_of(s * MOE_SUB, MOE_SUB), MOE_SUB)

    def x_copy(item, s):
        src = pl.ds(pl.multiple_of((win_ref[item] * spw + s) * MOE_SUB, MOE_SUB), MOE_SUB)
        return pltpu.make_async_copy(xs_ref.at[src], stage_s.at[sub_rows(s)], sem)

    def start_item(item):
        def body(s, carry):
            x_copy(item, s).start()
            return carry

        lax.fori_loop(lo_ref[item], hi_ref[item], body, 0)

    @pl.when(j == 0)
    def _():
        @pl.when(w == 0)
        def _():
            start_item(0)

        @pl.when(first_ref[w] == 1)
        def _():
            y_ref[...] = jnp.zeros(y_ref.shape, F32)

        def wait_one(s, carry):
            x_copy(w, s).wait()
            return carry

        def take(s, carry):
            rows = sub_rows(s)
            xb_s[rows, :] = stage_s[rows, :].astype(BF16)
            y_ref[rows, :] = jnp.broadcast_to(bd_ref[...], (MOE_SUB, d))
            return carry

        lax.fori_loop(lo, hi, wait_one, 0)
        lax.fori_loop(lo, hi, take, 0)

    @pl.when(jnp.logical_and(j == 1, w + 1 < NI))
    def _():
        start_item(w + 1)

    def mlp(r0, nrows):
        rows = pl.ds(r0, nrows)
        xb = xb_s[rows, :]
        hg = jnp.minimum(_dot(xb, wg_ref[...]) + bg_ref[...], SWIGLU_LIMIT)
        hl = jnp.clip(_dot(xb, wl_ref[...]) + bl_ref[...], -SWIGLU_LIMIT, SWIGLU_LIMIT)
        act = (hg * _sigmoid(SWIGLU_ALPHA * hg) * (hl + 1.0)).astype(BF16)
        y_ref[rows, :] += _dot(act, wd_ref[...])

    for g in range(1, spw + 1):
        @pl.when(hi - lo == g)
        def _(g=g):
            done = 0
            while done < g:
                step = min(2, g - done)
                mlp(pl.multiple_of((lo + done) * MOE_SUB, MOE_SUB), step * MOE_SUB)
                done += step


def _moe(plan, xs, w_gu, b_gu, w_dn, b_dn):
    P, d = xs.shape
    E, _, f2 = w_gu.shape
    f = f2 // 2
    nj = f // MOE_TF
    assert nj >= 2, "the next item's rows are prefetched at the second hidden tile"
    ni = plan[0].shape[0]
    grid_spec = pltpu.PrefetchScalarGridSpec(
        num_scalar_prefetch=5,
        grid=(ni, nj),
        in_specs=[
            pl.BlockSpec(memory_space=pl.ANY),
            pl.BlockSpec((None, d, MOE_TF), lambda w, j, win, e, lo, hi, fi: (e[w], 0, j)),
            pl.BlockSpec((None, d, MOE_TF), lambda w, j, win, e, lo, hi, fi: (e[w], 0, nj + j)),
            pl.BlockSpec((None, MOE_TF, d), lambda w, j, win, e, lo, hi, fi: (e[w], j, 0)),
            pl.BlockSpec((None, 1, MOE_TF), lambda w, j, win, e, lo, hi, fi: (e[w], 0, j)),
            pl.BlockSpec((None, 1, MOE_TF), lambda w, j, win, e, lo, hi, fi: (e[w], 0, nj + j)),
            pl.BlockSpec((None, 1, d), lambda w, j, win, e, lo, hi, fi: (e[w], 0, 0)),
        ],
        out_specs=pl.BlockSpec((MOE_WIN, d), lambda w, j, win, e, lo, hi, fi: (win[w], 0)),
        scratch_shapes=[
            pltpu.VMEM((MOE_WIN, d), F32),
            pltpu.VMEM((MOE_WIN, d), BF16),
            pltpu.SemaphoreType.DMA,
        ],
    )
    return pl.pallas_call(
        functools.partial(_moe_kernel, NI=ni, NJ=nj),
        out_shape=jax.ShapeDtypeStruct((P, d), F32),
        grid_spec=grid_spec,
        compiler_params=_cparams(("arbitrary", "arbitrary")),
        name="moe_experts",
    )(*plan, xs, w_gu, w_gu, w_dn, b_gu, b_gu, b_dn)


def _final_kernel(dest_ref, x1_ref, gate_ref, p_ref, wple_ref, wpg_ref, gpost_ref, gple_ref, gfin_ref,
                  y_ref, o_ref, yg_s, x3_s, emb_s, xg_s, sem, *, TM, NT, NC, final_norm):
    i = pl.program_id(0)
    slot = i % 2
    nslot = 1 - slot
    nxt = jnp.minimum(i + 1, NT - 1)

    def row_copy(tile, r, k, to_slot):
        d = dest_ref[(tile * TM + r) * TOP_K + k]
        return pltpu.make_async_copy(y_ref.at[pl.ds(d, 1)], yg_s.at[to_slot, k, pl.ds(r, 1)], sem.at[to_slot])

    def wait_slot(s):
        for k in range(TOP_K):
            pltpu.make_async_copy(y_ref.at[pl.ds(0, TM)], yg_s.at[s, k], sem.at[s]).wait()

    @pl.when(i == 0)
    def _():
        def issue(r, carry):
            for k in range(TOP_K):
                row_copy(0, r, k, 0).start()
            return carry

        lax.fori_loop(0, TM, issue, 0)

    wait_slot(slot)
    x2 = x1_ref[...]
    for k in range(TOP_K):
        x2 = x2 + gate_ref[:, k:k + 1] * yg_s[slot, k]
    x3_s[...] = x2
    xg_s[...] = _rms(x2, gple_ref[...]).astype(BF16)
    emb_s[...] = _rms(_dot(p_ref[...].astype(BF16), wple_ref[...]), gpost_ref[...])
    cw = x3_s.shape[1] // NC
    per = TM // NC
    for c in range(NC):
        cols = slice(c * cw, (c + 1) * cw)
        pg = _sigmoid(_dot(xg_s[...], wpg_ref[:, cols]))
        x3_s[:, cols] += pg * emb_s[:, cols]
        for r in range(c * per, (c + 1) * per):
            for k in range(TOP_K):
                row_copy(nxt, r, k, nslot).start()
    x3 = x3_s[...]
    if final_norm:
        x3 = _rms(x3, gfin_ref[...])
    o_ref[...] = x3

    @pl.when(i == NT - 1)
    def _():
        wait_slot(nslot)


def _final(dest, x1, gates, p2, w_ple, w_pg, g_post, g_ple, g_fin, y, tm, final_norm):
    n, d = x1.shape
    pd = p2.shape[1]
    const = lambda i, dest: (0, 0)
    grid_spec = pltpu.PrefetchScalarGridSpec(
        num_scalar_prefetch=1,
        grid=(n // tm,),
        in_specs=[
            pl.BlockSpec((tm, d), lambda i, dest: (i, 0)),
            pl.BlockSpec((tm, LANES), lambda i, dest: (i, 0)),
            pl.BlockSpec((tm, pd), lambda i, dest: (i, 0)),
            pl.BlockSpec((pd, d), const),
            pl.BlockSpec((d, d), const),
            pl.BlockSpec((1, d), const),
            pl.BlockSpec((1, d), const),
            pl.BlockSpec((1, d), const),
            pl.BlockSpec(memory_space=pl.ANY),
        ],
        out_specs=pl.BlockSpec((tm, d), lambda i, dest: (i, 0)),
        scratch_shapes=[
            pltpu.VMEM((2, TOP_K, tm, d), F32),
            pltpu.VMEM((tm, d), F32),
            pltpu.VMEM((tm, d), F32),
            pltpu.VMEM((tm, d), BF16),
            pltpu.SemaphoreType.DMA((2,)),
        ],
    )
    return pl.pallas_call(
        functools.partial(_final_kernel, TM=tm, NT=n // tm, NC=8, final_norm=final_norm),
        out_shape=jax.ShapeDtypeStruct((n, d), F32),
        grid_spec=grid_spec,
        compiler_params=_cparams(("arbitrary",)),
        name="combine_ple",
    )(dest, x1, gates, p2, w_ple, w_pg, g_post, g_ple, g_fin, y)


def _layer(x2, p2, batch, seq, g_mix, w_in, conv_w, conv_b, b_if, g_mh, g_sgu, w_s, b_s, w_br, w_out,
           g_ffn, w_router, b_router, w_gu, b_gu, w_dn, b_dn, g_ple, w_pg, w_ple, g_ple_post, g_final,
           final_norm):
    n, d = x2.shape
    H = M_HEADS
    E = w_router.shape[1]
    row = lambda v: v.reshape(1, -1).astype(F32)

    c_if = 3 * d
    w_if = jnp.pad(w_in[:, c_if:c_if + 2 * H], ((0, 0), (0, LANES - 2 * H))).astype(BF16)
    bias_if = jnp.pad(b_if.reshape(1, 2 * H), ((0, 0), (0, LANES - 2 * H))).astype(F32)

    tm_a = min(512, n)
    xn, z_if = _input_norm(x2, row(g_mix), w_if, bias_if, tm_a)

    tm = min(1024, n)
    tn = 1024
    n_plain = 2 * d // tn
    sig_tiles = tuple(range(2 * d // tn, 3 * d // tn)) + tuple(range(5 * d // tn, 7 * d // tn))
    f2 = w_gu.shape[2]
    z_main, w_gu16 = _input_proj(xn, w_in.T, w_gu.reshape(E * d, f2), 7 * d, tm, tn, n_plain, c_if // tn,
                                 sig_tiles, 2 * H)

    zif_t = z_if[:, :2 * H].T
    y_a, w_dn16 = _mlstm(z_main, zif_t, conv_w.astype(F32), row(conv_b), row(g_mh),
                         w_dn.reshape(-1, d), batch, seq, d)
    y_b = _sgu(z_main, w_s.astype(F32), b_s.T.astype(F32), row(g_sgu), d, min(512, n))
    mix = _merge(y_a, y_b, z_main, w_br.astype(BF16), d, tm, 512)

    w_r = jnp.pad(w_router, ((0, 0), (0, LANES - E))).astype(F32)
    b_r = jnp.pad(b_router.reshape(1, E), ((0, 0), (0, LANES - E))).astype(F32)
    x1, hp, idx, gates, cnt = _out_proj(x2, mix, w_out.astype(BF16), row(g_ffn), w_r, b_r, E, min(256, n))

    P = n * TOP_K + E * MOE_SUB
    P = -(-P // MOE_WIN) * MOE_WIN
    dest, padinfo, plan = _route_plan(idx[:, :TOP_K], idx[:, TOP_K:2 * TOP_K], cnt[0, :E], E, P)
    xs = _dispatch(dest, padinfo, hp, P, min(512, n), E)
    y = _moe(plan, xs, w_gu16.reshape(w_gu.shape), b_gu.reshape(E, 1, -1), w_dn16.reshape(w_dn.shape),
             b_dn.reshape(E, 1, -1))
    return _final(dest, x1, gates, p2, w_ple.astype(BF16), w_pg.astype(BF16), row(g_ple_post),
                  row(g_ple), row(g_final), y, min(256, n), final_norm)


def kernel(x, p, g_mix, w_in, conv_w, conv_b, b_if, g_mh, g_sgu, w_s, b_s, w_br, w_out, g_ffn, w_router,
           b_router, w_gu, b_gu, w_dn, b_dn, g_ple, w_pg, w_ple, g_ple_post, g_final):
    batch, seq, d = x.shape
    depth = p.shape[0]
    x2 = x.reshape(batch * seq, d)
    for i in range(depth):
        x2 = _layer(x2, p[i].reshape(batch * seq, -1), batch, seq, g_mix[i], w_in[i], conv_w[i], conv_b[i],
                    b_if[i], g_mh[i], g_sgu[i], w_s[i], b_s[i], w_br[i], w_out[i], g_ffn[i], w_router[i],
                    b_router[i], w_gu[i], b_gu[i], w_dn[i], b_dn[i], g_ple[i], w_pg[i], w_ple[i],
                    g_ple_post[i], g_final, final_norm=(i == depth - 1))
    return x2.reshape(batch, seq, d)
```

```python
import functools

import jax
import jax.numpy as jnp
from jax import lax
from jax.experimental import pallas as pl
from jax.experimental.pallas import tpu as pltpu

F32 = jnp.float32
BF16 = jnp.bfloat16
I32 = jnp.int32

RMS_EPS = 1e-6
M_HEADS = 8
CONV_W = 4
G_GROUPS = 8
G_CHUNK = 128
TOP_K = 4
SWIGLU_ALPHA = 1.702
SWIGLU_LIMIT = 7.0

LANES = 128
MLSTM_CHUNK = 256
MOE_SUB = 256
MOE_WIN = 1024
MOE_TF = 512
VMEM_LIMIT = 56 * 1024 * 1024


def _cparams(sem, vmem=VMEM_LIMIT):
    return pltpu.CompilerParams(dimension_semantics=sem, vmem_limit_bytes=vmem)


def _dot(a, b, **kw):
    return jnp.dot(a, b, preferred_element_type=F32, **kw)


def _rms(x, g):
    ms = jnp.mean(x * x, axis=-1, keepdims=True)
    return x * lax.rsqrt(ms + RMS_EPS) * g


def _sigmoid(x):
    return 0.5 * jnp.tanh(0.5 * x) + 0.5


def _norm_kernel(x_ref, g_ref, wif_ref, bif_ref, xn_ref, zif_ref):
    xn = _rms(x_ref[...], g_ref[...]).astype(BF16)
    xn_ref[...] = xn
    zif_ref[...] = _dot(xn, wif_ref[...]) + bif_ref[...]


def _input_norm(x2, g, w_if, b_if, tm):
    n, d = x2.shape
    return pl.pallas_call(
        _norm_kernel,
        out_shape=(jax.ShapeDtypeStruct((n, d), BF16), jax.ShapeDtypeStruct((n, LANES), F32)),
        grid=(n // tm,),
        in_specs=[
            pl.BlockSpec((tm, d), lambda i: (i, 0)),
            pl.BlockSpec((1, d), lambda i: (0, 0)),
            pl.BlockSpec((d, LANES), lambda i: (0, 0)),
            pl.BlockSpec((1, LANES), lambda i: (0, 0)),
        ],
        out_specs=(pl.BlockSpec((tm, d), lambda i: (i, 0)), pl.BlockSpec((tm, LANES), lambda i: (i, 0))),
        compiler_params=_cparams(("arbitrary",)),
        name="input_norm",
    )(x2, g, w_if, b_if)


def _inproj_kernel(x_ref, wa_ref, wb_ref, cin_ref, o_ref, cout_ref, w_s, *,
                   n_plain, n_aligned, sig_tiles, shift, chunks):
    j = pl.program_id(0)

    @pl.when(pl.program_id(1) == 0)
    def _():
        @pl.when(j < n_aligned)
        def _():
            w_s[...] = wa_ref[...].T.astype(BF16)

        @pl.when(j >= n_aligned)
        def _():
            w_s[...] = jnp.concatenate([wa_ref[shift:, :], wb_ref[...]], axis=0).T.astype(BF16)

    cm = x_ref.shape[0] // chunks

    def body(epilogue):
        cout_ref[...] = cin_ref[...].astype(BF16)
        for r in range(chunks):
            rows = slice(r * cm, (r + 1) * cm)
            o_ref[rows, :] = epilogue(_dot(x_ref[rows, :], w_s[...])).astype(o_ref.dtype)

    is_sig = functools.reduce(jnp.logical_or, [j == t for t in sig_tiles])
    is_plain = j < n_plain
    pl.when(is_plain)(lambda: body(lambda a: a))
    pl.when(is_sig)(lambda: body(_sigmoid))
    pl.when(jnp.logical_not(jnp.logical_or(is_plain, is_sig)))(lambda: body(jax.nn.gelu))


def _cast_blocks(rows, steps):
    cb = -(-rows // steps)
    cb = -(-cb // 16) * 16
    return cb, -(-rows // cb)


def _input_proj(xn, w_in_t, cast_src, c_out, tm, tn, n_plain, n_aligned, sig_tiles, shift):
    n, d = xn.shape
    ni = n // tm
    cast_rows, cast_cols = cast_src.shape
    cb, ncb = _cast_blocks(cast_rows, (c_out // tn) * ni)
    cast_map = lambda j, i: (jnp.minimum(j * ni + i, ncb - 1), 0)
    b_per_tile = tn // shift
    return pl.pallas_call(
        functools.partial(_inproj_kernel, n_plain=n_plain, n_aligned=n_aligned, sig_tiles=sig_tiles,
                          shift=shift, chunks=max(1, tm // 256)),
        out_shape=(jax.ShapeDtypeStruct((n, c_out), BF16), jax.ShapeDtypeStruct(cast_src.shape, BF16)),
        grid=(c_out // tn, ni),
        in_specs=[
            pl.BlockSpec((tm, d), lambda j, i: (i, 0)),
            pl.BlockSpec((tn, d), lambda j, i: (j, 0)),
            pl.BlockSpec((shift, d), lambda j, i: (jnp.maximum(j, n_aligned) * b_per_tile + b_per_tile, 0)),
            pl.BlockSpec((cb, cast_cols), cast_map),
        ],
        out_specs=(pl.BlockSpec((tm, tn), lambda j, i: (i, j)), pl.BlockSpec((cb, cast_cols), cast_map)),
        scratch_shapes=[pltpu.VMEM((d, tn), BF16)],
        compiler_params=_cparams(("arbitrary", "arbitrary")),
        name="input_proj",
    )(xn, w_in_t, w_in_t, cast_src)


def _log_sigmoid(x):
    return jnp.minimum(x, 0.0) - jnp.log1p(jnp.exp(-jnp.abs(x)))


def _cumsum_lanes(x):
    n = x.shape[1]
    lane = lax.broadcasted_iota(I32, x.shape, 1)
    sh = 1
    while sh < n:
        x = x + jnp.where(lane >= sh, pltpu.roll(x, sh, axis=1), 0.0)
        sh *= 2
    return x


def _mlstm_kernel(qk_ref, v_ref, o_ref, zr_ref, cw_ref, cb_ref, gmh_ref, cin_ref, y_ref, cout_ref,
                  ext_s, qk_s, c_s, m_s, g_s, *, L, H, DK, DV):
    cout_ref[...] = cin_ref[...].astype(BF16)

    @pl.when(pl.program_id(1) == 0)
    def _():
        ext_s[0:8, :] = jnp.zeros((8, ext_s.shape[1]), F32)
        c_s[...] = jnp.zeros(c_s.shape, F32)
        m_s[...] = jnp.zeros(m_s.shape, F32)
        g_s[...] = jnp.zeros(g_s.shape, F32)

    u = qk_ref[...].astype(F32)
    ext_s[8:8 + L, :] = u
    cw = cw_ref[...]
    conv = (cb_ref[...] + cw[3:4] * u + cw[2:3] * ext_s[7:7 + L, :]
            + cw[1:2] * ext_s[6:6 + L, :] + cw[0:1] * ext_s[5:5 + L, :])
    ext_s[0:8, :] = ext_s[L:L + 8, :]
    qk_s[...] = conv * _sigmoid(conv)

    zr = zr_ref[...]
    i_rows = zr[0:H]
    b_rows = _cumsum_lanes(_log_sigmoid(zr[H:2 * H]))
    g_s[0:H, :] = b_rows
    b_cols = g_s[...].T

    row = lax.broadcasted_iota(I32, (L, L), 0)
    col = lax.broadcasted_iota(I32, (L, L), 1)
    causal = row >= col
    ones_blk = jnp.where(lax.broadcasted_iota(I32, (L, LANES), 1) == 0, 1.0, 0.0).astype(BF16)

    for h in range(H):
        q = qk_s[:, h * DK:(h + 1) * DK]
        k = qk_s[:, (H + h) * DK:(H + h + 1) * DK] * (DK ** -0.5)
        kt = k.T
        b_col = b_cols[:, h:h + 1]
        b_row = b_rows[h:h + 1, :]
        i_row = i_rows[h:h + 1, :]
        m_prev = m_s[h:h + 1, 0:1]
        dmat = jnp.where(causal, b_col - b_row + i_row, -jnp.inf)
        m_inter = b_col + m_prev
        m_j = jnp.maximum(m_inter, jnp.max(dmat, axis=1, keepdims=True))
        q16 = q.astype(BF16)
        s = _dot(q16, kt.astype(BF16)) * jnp.exp(dmat - m_j)
        inter = jnp.exp(m_inter - m_j)
        vx = jnp.concatenate([v_ref[:, h * DV:(h + 1) * DV], ones_blk], axis=1)
        cx = c_s[h]
        nd = _dot(s.astype(BF16), vx) + inter * _dot(q16, cx.astype(BF16))
        den = nd[:, DV:DV + 1]
        hh = nd[:, 0:DV] / jnp.maximum(jnp.abs(den), jnp.exp(-m_j))
        b_last = b_row[:, L - 1:L]
        w_log = b_last - b_row + i_row
        m_new = jnp.maximum(b_last + m_prev, jnp.max(w_log, axis=1, keepdims=True))
        ktw = (kt * jnp.exp(w_log - m_new)).astype(BF16)
        c_s[h] = jnp.exp(b_last + m_prev - m_new) * cx + _dot(ktw, vx)
        m_s[h:h + 1, :] = jnp.broadcast_to(m_new, (1, LANES))
        yn = _rms(hh, gmh_ref[:, h * DV:(h + 1) * DV])
        y_ref[:, h * DV:(h + 1) * DV] = (yn * o_ref[:, h * DV:(h + 1) * DV].astype(F32)).astype(y_ref.dtype)


def _mlstm(z_main, zif_t, conv_w, conv_b, g_mh, cast_src, batch, seq, d):
    L, H = MLSTM_CHUNK, M_HEADS
    DV = d // H
    DK = DV // 2
    nc = seq // L
    n = batch * seq
    row_blk = lambda b, c: b * nc + c
    cast_rows, cast_cols = cast_src.shape
    cb, ncb = _cast_blocks(cast_rows, batch * nc)
    cast_map = lambda b, c: (jnp.minimum(row_blk(b, c), ncb - 1), 0)
    return pl.pallas_call(
        functools.partial(_mlstm_kernel, L=L, H=H, DK=DK, DV=DV),
        out_shape=(jax.ShapeDtypeStruct((n, d), BF16), jax.ShapeDtypeStruct(cast_src.shape, BF16)),
        grid=(batch, nc),
        in_specs=[
            pl.BlockSpec((L, d), lambda b, c: (row_blk(b, c), 0)),
            pl.BlockSpec((L, d), lambda b, c: (row_blk(b, c), 1)),
            pl.BlockSpec((L, d), lambda b, c: (row_blk(b, c), 2)),
            pl.BlockSpec((2 * H, L), lambda b, c: (0, row_blk(b, c))),
            pl.BlockSpec((CONV_W, d), lambda b, c: (0, 0)),
            pl.BlockSpec((1, d), lambda b, c: (0, 0)),
            pl.BlockSpec((1, d), lambda b, c: (0, 0)),
            pl.BlockSpec((cb, cast_cols), cast_map),
        ],
        out_specs=(pl.BlockSpec((L, d), lambda b, c: (row_blk(b, c), 0)),
                   pl.BlockSpec((cb, cast_cols), cast_map)),
        scratch_shapes=[
            pltpu.VMEM((L + 8, d), F32),
            pltpu.VMEM((L, d), F32),
            pltpu.VMEM((H, DK, DV + LANES), F32),
            pltpu.VMEM((H, LANES), F32),
            pltpu.VMEM((LANES, L), F32),
        ],
        compiler_params=_cparams(("arbitrary", "arbitrary")),
        name="mlstm",
    )(z_main, z_main, z_main, zif_t, conv_w, conv_b, g_mh, cast_src)


def _sgu_kernel(u_ref, v_ref, ws_ref, bst_ref, g_ref, y_ref, vn_s, *, R, DG):
    vn_s[...] = _rms(v_ref[...].astype(F32), g_ref[...]).astype(BF16)
    row = lax.broadcasted_iota(I32, (G_CHUNK, G_CHUNK), 0)
    col = lax.broadcasted_iota(I32, (G_CHUNK, G_CHUNK), 1)
    for g in range(G_GROUPS):
        ws = jnp.where(row >= col, ws_ref[g], 0.0).astype(BF16)
        bias = bst_ref[:, g:g + 1]
        for c in range(R // G_CHUNK):
            rs = slice(c * G_CHUNK, (c + 1) * G_CHUNK)
            cs = slice(g * DG, (g + 1) * DG)
            vm = _dot(ws, vn_s[rs, cs]) + bias
            y_ref[rs, cs] = (u_ref[rs, cs].astype(F32) * vm).astype(y_ref.dtype)


def _sgu(z_main, w_s, b_s_t, g_sgu, d, R):
    n = z_main.shape[0]
    return pl.pallas_call(
        functools.partial(_sgu_kernel, R=R, DG=d // G_GROUPS),
        out_shape=jax.ShapeDtypeStruct((n, d), BF16),
        grid=(n // R,),
        in_specs=[
            pl.BlockSpec((R, d), lambda i: (i, 3)),
            pl.BlockSpec((R, d), lambda i: (i, 4)),
            pl.BlockSpec((G_GROUPS, G_CHUNK, G_CHUNK), lambda i: (0, 0, 0)),
            pl.BlockSpec((G_CHUNK, G_GROUPS), lambda i: (0, 0)),
            pl.BlockSpec((1, d), lambda i: (0, 0)),
        ],
        out_specs=pl.BlockSpec((R, d), lambda i: (i, 0)),
        scratch_shapes=[pltpu.VMEM((R, d), BF16)],
        compiler_params=_cparams(("arbitrary",)),
        name="spatial_gating",
    )(z_main, z_main, w_s, b_s_t, g_sgu)


def _merge_kernel(ya_ref, yb_ref, ga_ref, gb_ref, w0_ref, w1_ref, o_ref):
    a = _dot(ya_ref[...], w0_ref[...])
    b = _dot(yb_ref[...], w1_ref[...])
    o_ref[...] = (ga_ref[...].astype(F32) * a + gb_ref[...].astype(F32) * b).astype(o_ref.dtype)


def _merge(y_a, y_b, z_main, w_br, d, tm, tn):
    n = y_a.shape[0]
    ga0 = 5 * d // tn
    gb0 = 6 * d // tn
    return pl.pallas_call(
        _merge_kernel,
        out_shape=jax.ShapeDtypeStruct((n, d), BF16),
        grid=(d // tn, n // tm),
        in_specs=[
            pl.BlockSpec((tm, d), lambda j, i: (i, 0)),
            pl.BlockSpec((tm, d), lambda j, i: (i, 0)),
            pl.BlockSpec((tm, tn), lambda j, i: (i, ga0 + j)),
            pl.BlockSpec((tm, tn), lambda j, i: (i, gb0 + j)),
            pl.BlockSpec((None, d, tn), lambda j, i: (0, 0, j)),
            pl.BlockSpec((None, d, tn), lambda j, i: (1, 0, j)),
        ],
        out_specs=pl.BlockSpec((tm, tn), lambda j, i: (i, j)),
        compiler_params=_cparams(("arbitrary", "arbitrary")),
        name="branch_merge",
    )(y_a, y_b, z_main, z_main, w_br, w_br)


def _split_bf16(x):
    hi = x.astype(BF16)
    return hi, (x - hi.astype(F32)).astype(BF16)


def _outproj_kernel(x_ref, mix_ref, wo_ref, gf_ref, wr_ref, br_ref,
                    x1_ref, hp_ref, idx_ref, gate_ref, cnt_ref, cnt_s, *, E):
    @pl.when(pl.program_id(0) == 0)
    def _():
        cnt_s[...] = jnp.zeros(cnt_s.shape, F32)

    x1 = x_ref[...] + _dot(mix_ref[...], wo_ref[...])
    x1_ref[...] = x1
    hn = _rms(x1, gf_ref[...])
    hp_ref[...] = hn
    h_hi, h_lo = _split_bf16(hn)
    w_hi, w_lo = _split_bf16(wr_ref[...])
    logits = _dot(h_hi, w_hi) + (_dot(h_hi, w_lo) + _dot(h_lo, w_hi)) + br_ref[...]
    lane = lax.broadcasted_iota(I32, logits.shape, 1)
    work = jnp.where(lane < E, logits, -jnp.inf)
    vals, idxs = [], []
    for _ in range(TOP_K):
        mx = jnp.max(work, axis=1, keepdims=True)
        ix = jnp.min(jnp.where(work == mx, lane, LANES), axis=1, keepdims=True)
        vals.append(mx)
        idxs.append(ix)
        work = jnp.where(lane == ix, -jnp.inf, work)
    exps = [jnp.exp(v - vals[0]) for v in vals]
    inv = 1.0 / functools.reduce(jnp.add, exps)
    tm = logits.shape[0]
    onehots = [jnp.where(lane == ix, 1.0, 0.0) for ix in idxs]
    oh_sum = functools.reduce(jnp.add, onehots)
    earlier = lax.broadcasted_iota(I32, (tm, tm), 0) > lax.broadcasted_iota(I32, (tm, tm), 1)
    before = _dot(jnp.where(earlier, 1.0, 0.0).astype(BF16), oh_sum.astype(BF16)) + cnt_s[0:1, :]
    cnt = cnt_s[0:1, :] + jnp.sum(oh_sum, axis=0, keepdims=True)
    cnt_s[...] = jnp.broadcast_to(cnt, cnt_s.shape)
    cnt_ref[...] = jnp.broadcast_to(cnt, cnt_s.shape).astype(I32)
    idx_out = jnp.zeros(logits.shape, I32)
    gate_out = jnp.zeros(logits.shape, F32)
    for k in range(TOP_K):
        rank = jnp.sum(onehots[k] * before, axis=1, keepdims=True).astype(I32)
        idx_out = jnp.where(lane == k, idxs[k], idx_out)
        idx_out = jnp.where(lane == TOP_K + k, rank, idx_out)
        gate_out = jnp.where(lane == k, exps[k] * inv, gate_out)
    idx_ref[...] = idx_out
    gate_ref[...] = gate_out


def _out_proj(x2, mix, w_out, g_ffn, w_router, b_router, E, tm):
    n, d = x2.shape
    return pl.pallas_call(
        functools.partial(_outproj_kernel, E=E),
        out_shape=(
            jax.ShapeDtypeStruct((n, d), F32),
            jax.ShapeDtypeStruct((n, d), F32),
            jax.ShapeDtypeStruct((n, LANES), I32),
            jax.ShapeDtypeStruct((n, LANES), F32),
            jax.ShapeDtypeStruct((8, LANES), I32),
        ),
        grid=(n // tm,),
        in_specs=[
            pl.BlockSpec((tm, d), lambda i: (i, 0)),
            pl.BlockSpec((tm, d), lambda i: (i, 0)),
            pl.BlockSpec((d, d), lambda i: (0, 0)),
            pl.BlockSpec((1, d), lambda i: (0, 0)),
            pl.BlockSpec((d, LANES), lambda i: (0, 0)),
            pl.BlockSpec((1, LANES), lambda i: (0, 0)),
        ],
        out_specs=(
            pl.BlockSpec((tm, d), lambda i: (i, 0)),
            pl.BlockSpec((tm, d), lambda i: (i, 0)),
            pl.BlockSpec((tm, LANES), lambda i: (i, 0)),
            pl.BlockSpec((tm, LANES), lambda i: (i, 0)),
            pl.BlockSpec((8, LANES), lambda i: (0, 0)),
        ),
        scratch_shapes=[pltpu.VMEM((8, LANES), F32)],
        compiler_params=_cparams(("arbitrary",)),
        name="out_proj_router",
    )(x2, mix, w_out, g_ffn, w_router, b_router)


def _route_plan(top_idx, rank, counts, E, P):
    n, k = top_idx.shape
    a = n * k
    e_flat = top_idx.reshape(a)
    padded = ((counts + MOE_SUB - 1) // MOE_SUB) * MOE_SUB
    pend = jnp.cumsum(padded)
    pstart = pend - padded
    onehot = e_flat[:, None] == jnp.arange(E, dtype=I32)[None, :]
    dest = (rank.reshape(a) + jnp.sum(jnp.where(onehot, pstart[None, :], 0), axis=1)).astype(I32)

    nsb = P // MOE_SUB
    spw = MOE_WIN // MOE_SUB
    ni = P // MOE_WIN + E
    sb = jnp.arange(nsb, dtype=I32)
    sb_e = jnp.sum((pend[None, :] <= (sb * MOE_SUB)[:, None]).astype(I32), axis=1)
    sb_e = jnp.minimum(sb_e, E - 1)
    valid = sb * MOE_SUB < pend[-1]
    sb_e = jnp.where(valid, sb_e, sb_e[pend[-1] // MOE_SUB - 1])
    prev_e = jnp.concatenate([jnp.full((1,), -1, I32), sb_e[:-1]])
    is_start = (sb % spw == 0) | (sb_e != prev_e)
    item_of_sb = jnp.cumsum(is_start.astype(I32)) - 1
    n_items = jnp.sum(is_start.astype(I32))
    slot = jnp.where(is_start, item_of_sb, ni)
    item_win = jnp.zeros((ni,), I32).at[slot].set(sb // spw, mode="drop")
    item_e = jnp.zeros((ni,), I32).at[slot].set(sb_e, mode="drop")
    item_lo = jnp.zeros((ni,), I32).at[slot].set(sb % spw, mode="drop")
    win_has_unused = jnp.zeros((nsb // spw,), I32).at[sb // spw].add(1 - valid.astype(I32)) > 0
    zero_win = (sb % spw == 0) & win_has_unused[sb // spw]
    item_first = jnp.zeros((ni,), I32).at[slot].set(zero_win.astype(I32), mode="drop")
    item_len = jnp.zeros((ni,), I32).at[jnp.where(valid, item_of_sb, ni)].add(1, mode="drop")
    it = jnp.arange(ni, dtype=I32)
    live = it < n_items
    last = jnp.maximum(n_items - 1, 0)
    src = jnp.minimum(it, last)
    item_win = item_win[src]
    item_e = item_e[src]
    item_lo = jnp.where(live, item_lo, 0)
    item_hi = jnp.where(live, item_lo + item_len, 0)
    item_first = jnp.where(live, item_first, 0)
    padinfo = jnp.concatenate([pstart + counts, padded - counts, pend[-1:] // MOE_SUB]).astype(I32)
    return dest, padinfo, (item_win, item_e, item_lo, item_hi, item_first)


def _dispatch_kernel(dest_ref, pad_ref, hp_ref, xs_ref, sem, pad_sem, tail_sem, *, G, E):
    base = pl.program_id(0) * G

    def row_copy(r, d, s):
        return pltpu.make_async_copy(hp_ref.at[pl.ds(r, 1)], xs_ref.at[pl.ds(d, 1)], s)

    def issue(r, carry):
        for k in range(TOP_K):
            row_copy(r, dest_ref[(base + r) * TOP_K + k], sem).start()
        return carry

    lax.fori_loop(0, G, issue, 0)

    @pl.when(pl.program_id(0) == 0)
    def _():
        def fill(e, carry):
            start = pad_ref[e]

            def one(r, c):
                row_copy(0, start + r, pad_sem).start()
                return c

            return lax.fori_loop(0, pad_ref[E + e], one, carry)

        def drain(e, carry):
            def one(r, c):
                row_copy(0, 0, pad_sem).wait()
                return c

            return lax.fori_loop(0, pad_ref[E + e], one, carry)

        def tail_copy(s):
            rows = pl.ds(pl.multiple_of(s * MOE_SUB, MOE_SUB), MOE_SUB)
            return pltpu.make_async_copy(hp_ref.at[pl.ds(0, MOE_SUB)], xs_ref.at[rows], tail_sem)

        def tail_fill(s, carry):
            tail_copy(s).start()
            return carry

        def tail_drain(s, carry):
            tail_copy(s).wait()
            return carry

        n_sub = xs_ref.shape[0] // MOE_SUB
        lax.fori_loop(0, E, fill, 0)
        lax.fori_loop(pad_ref[2 * E], n_sub, tail_fill, 0)
        lax.fori_loop(0, E, drain, 0)
        lax.fori_loop(pad_ref[2 * E], n_sub, tail_drain, 0)

    pltpu.make_async_copy(xs_ref.at[pl.ds(0, G * TOP_K)], xs_ref.at[pl.ds(0, G * TOP_K)], sem).wait()


def _dispatch(dest, padinfo, hp, P, G, E):
    n, w = hp.shape
    grid_spec = pltpu.PrefetchScalarGridSpec(
        num_scalar_prefetch=2,
        grid=(n // G,),
        in_specs=[pl.BlockSpec((G, w), lambda i, dest, pad: (i, 0))],
        out_specs=pl.BlockSpec(memory_space=pl.ANY),
        scratch_shapes=[pltpu.SemaphoreType.DMA] * 3,
    )
    return pl.pallas_call(
        functools.partial(_dispatch_kernel, G=G, E=E),
        out_shape=jax.ShapeDtypeStruct((P, w), F32),
        grid_spec=grid_spec,
        compiler_params=_cparams(("arbitrary",)),
        name="moe_dispatch",
    )(dest, padinfo, hp)


def _moe_kernel(win_ref, e_ref, lo_ref, hi_ref, first_ref,
                xs_ref, wg_ref, wl_ref, wd_ref, bg_ref, bl_ref, bd_ref, y_ref,
                stage_s, xb_s, sem, *, NI, NJ):
    del e_ref
    w = pl.program_id(0)
    j = pl.program_id(1)
    lo = lo_ref[w]
    hi = hi_ref[w]
    spw = MOE_WIN // MOE_SUB

    def sub_rows(s):
        return pl.ds(pl.multiple_of(s * MOE_SUB, MOE_SUB), MOE_SUB)

    def x_copy(item, s):
        src = pl.ds(pl.multiple_of((win_ref[item] * spw + s) * MOE_SUB, MOE_SUB), MOE_SUB)
        return pltpu.make_async_copy(xs_ref.at[src], stage_s.at[sub_rows(s)], sem)

    def start_item(item):
        def body(s, carry):
            x_copy(item, s).start()
            return carry

        lax.fori_loop(lo_ref[item], hi_ref[item], body, 0)

    @pl.when(j == 0)
    def _():
        @pl.when(w == 0)
        def _():
            start_item(0)

        @pl.when(first_ref[w] == 1)
        def _():
            y_ref[...] = jnp.zeros(y_ref.shape, F32)

        def wait_one(s, carry):
            x_copy(w, s).wait()
            return carry

        lax.fori_loop(lo, hi, wait_one, 0)

    @pl.when(jnp.logical_and(j == 1, w + 1 < NI))
    def _():
        start_item(w + 1)

    def mlp(r0, nrows, first_tile):
        rows = pl.ds(r0, nrows)
        if first_tile:
            xb = stage_s[rows, :].astype(BF16)
            xb_s[rows, :] = xb
        else:
            xb = xb_s[rows, :]
        hg = jnp.minimum(_dot(xb, wg_ref[...]) + bg_ref[...], SWIGLU_LIMIT)
        hl = jnp.clip(_dot(xb, wl_ref[...]) + bl_ref[...], -SWIGLU_LIMIT, SWIGLU_LIMIT)
        act = (hg * _sigmoid(SWIGLU_ALPHA * hg) * (hl + 1.0)).astype(BF16)
        out = _dot(act, wd_ref[...])
        if first_tile:
            y_ref[rows, :] = out + bd_ref[...]
        else:
            y_ref[rows, :] += out

    for g in range(1, spw + 1):
        for first_tile in (True, False):
            @pl.when(jnp.logical_and(hi - lo == g, (j == 0) if first_tile else (j > 0)))
            def _(g=g, first_tile=first_tile):
                done = 0
                while done < g:
                    step = min(2, g - done)
                    mlp(pl.multiple_of((lo + done) * MOE_SUB, MOE_SUB), step * MOE_SUB, first_tile)
                    done += step


def _moe(plan, xs, w_gu, b_gu, w_dn, b_dn):
    P, d = xs.shape
    E, _, f2 = w_gu.shape
    f = f2 // 2
    nj = f // MOE_TF
    assert nj >= 2, "the next item's rows are prefetched at the second hidden tile"
    ni = plan[0].shape[0]
    grid_spec = pltpu.PrefetchScalarGridSpec(
        num_scalar_prefetch=5,
        grid=(ni, nj),
        in_specs=[
            pl.BlockSpec(memory_space=pl.ANY),
            pl.BlockSpec((None, d, MOE_TF), lambda w, j, win, e, lo, hi, fi: (e[w], 0, j)),
            pl.BlockSpec((None, d, MOE_TF), lambda w, j, win, e, lo, hi, fi: (e[w], 0, nj + j)),
            pl.BlockSpec((None, MOE_TF, d), lambda w, j, win, e, lo, hi, fi: (e[w], j, 0)),
            pl.BlockSpec((None, 1, MOE_TF), lambda w, j, win, e, lo, hi, fi: (e[w], 0, j)),
            pl.BlockSpec((None, 1, MOE_TF), lambda w, j, win, e, lo, hi, fi: (e[w], 0, nj + j)),
            pl.BlockSpec((None, 1, d), lambda w, j, win, e, lo, hi, fi: (e[w], 0, 0)),
        ],
        out_specs=pl.BlockSpec((MOE_WIN, d), lambda w, j, win, e, lo, hi, fi: (win[w], 0)),
        scratch_shapes=[
            pltpu.VMEM((MOE_WIN, d), F32),
            pltpu.VMEM((MOE_WIN, d), BF16),
            pltpu.SemaphoreType.DMA,
        ],
    )
    return pl.pallas_call(
        functools.partial(_moe_kernel, NI=ni, NJ=nj),
        out_shape=jax.ShapeDtypeStruct((P, d), F32),
        grid_spec=grid_spec,
        compiler_params=_cparams(("arbitrary", "arbitrary")),
        name="moe_experts",
    )(*plan, xs, w_gu, w_gu, w_dn, b_gu, b_gu, b_dn)


def _final_kernel(dest_ref, x1_ref, gate_ref, p_ref, wple_ref, wpg_ref, gpost_ref, gple_ref, gfin_ref,
                  y_ref, o_ref, yg_s, x3_s, emb_s, xg_s, sem, *, TM, NT, NC, final_norm):
    i = pl.program_id(0)
    slot = i % 2
    nslot = 1 - slot
    nxt = jnp.minimum(i + 1, NT - 1)

    def row_copy(tile, r, k, to_slot):
        d = dest_ref[(tile * TM + r) * TOP_K + k]
        return pltpu.make_async_copy(y_ref.at[pl.ds(d, 1)], yg_s.at[to_slot, k, pl.ds(r, 1)], sem.at[to_slot])

    def wait_slot(s):
        for k in range(TOP_K):
            pltpu.make_async_copy(y_ref.at[pl.ds(0, TM)], yg_s.at[s, k], sem.at[s]).wait()

    @pl.when(i == 0)
    def _():
        def issue(r, carry):
            for k in range(TOP_K):
                row_copy(0, r, k, 0).start()
            return carry

        lax.fori_loop(0, TM, issue, 0)

    wait_slot(slot)
    x2 = x1_ref[...]
    for k in range(TOP_K):
        x2 = x2 + gate_ref[:, k:k + 1] * yg_s[slot, k]
    x3_s[...] = x2
    xg_s[...] = _rms(x2, gple_ref[...]).astype(BF16)
    emb_s[...] = _rms(_dot(p_ref[...].astype(BF16), wple_ref[...]), gpost_ref[...])
    cw = x3_s.shape[1] // NC
    per = TM // NC
    for c in range(NC):
        cols = slice(c * cw, (c + 1) * cw)
        pg = _sigmoid(_dot(xg_s[...], wpg_ref[:, cols]))
        x3_s[:, cols] += pg * emb_s[:, cols]
        for r in range(c * per, (c + 1) * per):
            for k in range(TOP_K):
                row_copy(nxt, r, k, nslot).start()
    x3 = x3_s[...]
    if final_norm:
        x3 = _rms(x3, gfin_ref[...])
    o_ref[...] = x3

    @pl.when(i == NT - 1)
    def _():
        wait_slot(nslot)


def _final(dest, x1, gates, p2, w_ple, w_pg, g_post, g_ple, g_fin, y, tm, final_norm):
    n, d = x1.shape
    pd = p2.shape[1]
    const = lambda i, dest: (0, 0)
    grid_spec = pltpu.PrefetchScalarGridSpec(
        num_scalar_prefetch=1,
        grid=(n // tm,),
        in_specs=[
            pl.BlockSpec((tm, d), lambda i, dest: (i, 0)),
            pl.BlockSpec((tm, LANES), lambda i, dest: (i, 0)),
            pl.BlockSpec((tm, pd), lambda i, dest: (i, 0)),
            pl.BlockSpec((pd, d), const),
            pl.BlockSpec((d, d), const),
            pl.BlockSpec((1, d), const),
            pl.BlockSpec((1, d), const),
            pl.BlockSpec((1, d), const),
            pl.BlockSpec(memory_space=pl.ANY),
        ],
        out_specs=pl.BlockSpec((tm, d), lambda i, dest: (i, 0)),
        scratch_shapes=[
            pltpu.VMEM((2, TOP_K, tm, d), F32),
            pltpu.VMEM((tm, d), F32),
            pltpu.VMEM((tm, d), F32),
            pltpu.VMEM((tm, d), BF16),
            pltpu.SemaphoreType.DMA((2,)),
        ],
    )
    return pl.pallas_call(
        functools.partial(_final_kernel, TM=tm, NT=n // tm, NC=8, final_norm=final_norm),
        out_shape=jax.ShapeDtypeStruct((n, d), F32),
        grid_spec=grid_spec,
        compiler_params=_cparams(("arbitrary",)),
        name="combine_ple",
    )(dest, x1, gates, p2, w_ple, w_pg, g_post, g_ple, g_fin, y)


def _layer(x2, p2, batch, seq, g_mix, w_in, conv_w, conv_b, b_if, g_mh, g_sgu, w_s, b_s, w_br, w_out,
           g_ffn, w_router, b_router, w_gu, b_gu, w_dn, b_dn, g_ple, w_pg, w_ple, g_ple_post, g_final,
           final_norm):
    n, d = x2.shape
    H = M_HEADS
    E = w_router.shape[1]
    row = lambda v: v.reshape(1, -1).astype(F32)

    c_if = 3 * d
    w_if = jnp.pad(w_in[:, c_if:c_if + 2 * H], ((0, 0), (0, LANES - 2 * H))).astype(BF16)
    bias_if = jnp.pad(b_if.reshape(1, 2 * H), ((0, 0), (0, LANES - 2 * H))).astype(F32)

    tm_a = min(512, n)
    xn, z_if = _input_norm(x2, row(g_mix), w_if, bias_if, tm_a)

    tm = min(1024, n)
    tn = 1024
    n_plain = 2 * d // tn
    sig_tiles = tuple(range(2 * d // tn, 3 * d // tn)) + tuple(range(5 * d // tn, 7 * d // tn))
    f2 = w_gu.shape[2]
    z_main, w_gu16 = _input_proj(xn, w_in.T, w_gu.reshape(E * d, f2), 7 * d, tm, tn, n_plain, c_if // tn,
                                 sig_tiles, 2 * H)

    zif_t = z_if[:, :2 * H].T
    y_a, w_dn16 = _mlstm(z_main, zif_t, conv_w.astype(F32), row(conv_b), row(g_mh),
                         w_dn.reshape(-1, d), batch, seq, d)
    y_b = _sgu(z_main, w_s.astype(F32), b_s.T.astype(F32), row(g_sgu), d, min(512, n))
    mix = _merge(y_a, y_b, z_main, w_br.astype(BF16), d, tm, 512)

    w_r = jnp.pad(w_router, ((0, 0), (0, LANES - E))).astype(F32)
    b_r = jnp.pad(b_router.reshape(1, E), ((0, 0), (0, LANES - E))).astype(F32)
    x1, hp, idx, gates, cnt = _out_proj(x2, mix, w_out.astype(BF16), row(g_ffn), w_r, b_r, E, min(256, n))

    P = n * TOP_K + E * MOE_SUB
    P = -(-P // MOE_WIN) * MOE_WIN
    dest, padinfo, plan = _route_plan(idx[:, :TOP_K], idx[:, TOP_K:2 * TOP_K], cnt[0, :E], E, P)
    xs = _dispatch(dest, padinfo, hp, P, min(512, n), E)
    y = _moe(plan, xs, w_gu16.reshape(w_gu.shape), b_gu.reshape(E, 1, -1), w_dn16.reshape(w_dn.shape),
             b_dn.reshape(E, 1, -1))
    return _final(dest, x1, gates, p2, w_ple.astype(BF16), w_pg.astype(BF16), row(g_ple_post),
                  row(g_ple), row(g_final), y, min(256, n), final_norm)


def kernel(x, p, g_mix, w_in, conv_w, conv_b, b_if, g_mh, g_sgu, w_s, b_s, w_br, w_out, g_ffn, w_router,
           b_router, w_gu, b_gu, w_dn, b_dn, g_ple, w_pg, w_ple, g_ple_post, g_final):
    batch, seq, d = x.shape
    depth = p.shape[0]
    x2 = x.reshape(batch * seq, d)
    for i in range(depth):
        x2 = _layer(x2, p[i].reshape(batch * seq, -1), batch, seq, g_mix[i], w_in[i], conv_w[i], conv_b[i],
                    b_if[i], g_mh[i], g_sgu[i], w_s[i], b_s[i], w_br[i], w_out[i], g_ffn[i], w_router[i],
                    b_router[i], w_gu[i], b_gu[i], w_dn[i], b_dn[i], g_ple[i], w_pg[i], w_ple[i],
                    g_ple_post[i], g_final, final_norm=(i == depth - 1))
    return x2.reshape(batch, seq, d)
```

```python
import functools

import jax
import jax.numpy as jnp
from jax import lax
from jax.experimental import pallas as pl
from jax.experimental.pallas import tpu as pltpu

F32 = jnp.float32
BF16 = jnp.bfloat16
I32 = jnp.int32

RMS_EPS = 1e-6
M_HEADS = 8
CONV_W = 4
G_GROUPS = 8
G_CHUNK = 128
TOP_K = 4
SWIGLU_ALPHA = 1.702
SWIGLU_LIMIT = 7.0

LANES = 128
MLSTM_CHUNK = 256
MOE_SUB = 256
MOE_WIN = 1024
MOE_TF = 512
VMEM_LIMIT = 56 * 1024 * 1024


def _cparams(sem, vmem=VMEM_LIMIT):
    return pltpu.CompilerParams(dimension_semantics=sem, vmem_limit_bytes=vmem)


def _dot(a, b, **kw):
    return jnp.dot(a, b, preferred_element_type=F32, **kw)


def _rms(x, g):
    ms = jnp.mean(x * x, axis=-1, keepdims=True)
    return x * lax.rsqrt(ms + RMS_EPS) * g


def _sigmoid(x):
    return 0.5 * jnp.tanh(0.5 * x) + 0.5


def _norm_kernel(x_ref, g_ref, wif_ref, bif_ref, xn_ref, zif_ref):
    xn = _rms(x_ref[...], g_ref[...]).astype(BF16)
    xn_ref[...] = xn
    zif_ref[...] = _dot(xn, wif_ref[...]) + bif_ref[...]


def _input_norm(x2, g, w_if, b_if, tm):
    n, d = x2.shape
    return pl.pallas_call(
        _norm_kernel,
        out_shape=(jax.ShapeDtypeStruct((n, d), BF16), jax.ShapeDtypeStruct((n, LANES), F32)),
        grid=(n // tm,),
        in_specs=[
            pl.BlockSpec((tm, d), lambda i: (i, 0)),
            pl.BlockSpec((1, d), lambda i: (0, 0)),
            pl.BlockSpec((d, LANES), lambda i: (0, 0)),
            pl.BlockSpec((1, LANES), lambda i: (0, 0)),
        ],
        out_specs=(pl.BlockSpec((tm, d), lambda i: (i, 0)), pl.BlockSpec((tm, LANES), lambda i: (i, 0))),
        compiler_params=_cparams(("arbitrary",)),
        name="input_norm",
    )(x2, g, w_if, b_if)


def _inproj_kernel(x_ref, wa_ref, wb_ref, cin_ref, o_ref, cout_ref, w_s, *,
                   n_plain, n_aligned, sig_tiles, shift, chunks):
    j = pl.program_id(0)

    @pl.when(pl.program_id(1) == 0)
    def _():
        @pl.when(j < n_aligned)
        def _():
            w_s[...] = wa_ref[...].T.astype(BF16)

        @pl.when(j >= n_aligned)
        def _():
            w_s[...] = jnp.concatenate([wa_ref[shift:, :], wb_ref[...]], axis=0).T.astype(BF16)

    cm = x_ref.shape[0] // chunks

    def body(epilogue):
        cout_ref[...] = cin_ref[...].astype(BF16)
        for r in range(chunks):
            rows = slice(r * cm, (r + 1) * cm)
            o_ref[rows, :] = epilogue(_dot(x_ref[rows, :], w_s[...])).astype(o_ref.dtype)

    is_sig = functools.reduce(jnp.logical_or, [j == t for t in sig_tiles])
    is_plain = j < n_plain
    pl.when(is_plain)(lambda: body(lambda a: a))
    pl.when(is_sig)(lambda: body(_sigmoid))
    pl.when(jnp.logical_not(jnp.logical_or(is_plain, is_sig)))(lambda: body(jax.nn.gelu))


def _cast_blocks(rows, steps):
    cb = -(-rows // steps)
    cb = -(-cb // 16) * 16
    return cb, -(-rows // cb)


def _input_proj(xn, w_in_t, cast_src, c_out, tm, tn, n_plain, n_aligned, sig_tiles, shift):
    n, d = xn.shape
    ni = n // tm
    cast_rows, cast_cols = cast_src.shape
    cb, ncb = _cast_blocks(cast_rows, (c_out // tn) * ni)
    cast_map = lambda j, i: (jnp.minimum(j * ni + i, ncb - 1), 0)
    b_per_tile = tn // shift
    return pl.pallas_call(
        functools.partial(_inproj_kernel, n_plain=n_plain, n_aligned=n_aligned, sig_tiles=sig_tiles,
                          shift=shift, chunks=max(1, tm // 256)),
        out_shape=(jax.ShapeDtypeStruct((n, c_out), BF16), jax.ShapeDtypeStruct(cast_src.shape, BF16)),
        grid=(c_out // tn, ni),
        in_specs=[
            pl.BlockSpec((tm, d), lambda j, i: (i, 0)),
            pl.BlockSpec((tn, d), lambda j, i: (j, 0)),
            pl.BlockSpec((shift, d), lambda j, i: (jnp.maximum(j, n_aligned) * b_per_tile + b_per_tile, 0)),
            pl.BlockSpec((cb, cast_cols), cast_map),
        ],
        out_specs=(pl.BlockSpec((tm, tn), lambda j, i: (i, j)), pl.BlockSpec((cb, cast_cols), cast_map)),
        scratch_shapes=[pltpu.VMEM((d, tn), BF16)],
        compiler_params=_cparams(("arbitrary", "arbitrary")),
        name="input_proj",
    )(xn, w_in_t, w_in_t, cast_src)


def _log_sigmoid(x):
    return jnp.minimum(x, 0.0) - jnp.log1p(jnp.exp(-jnp.abs(x)))


def _cumsum_lanes(x):
    n = x.shape[1]
    lane = lax.broadcasted_iota(I32, x.shape, 1)
    sh = 1
    while sh < n:
        x = x + jnp.where(lane >= sh, pltpu.roll(x, sh, axis=1), 0.0)
        sh *= 2
    return x


def _mlstm_kernel(qk_ref, v_ref, o_ref, zr_ref, cw_ref, cb_ref, gmh_ref, cin_ref, y_ref, cout_ref,
                  ext_s, qk_s, c_s, m_s, g_s, *, L, H, DK, DV):
    cout_ref[...] = cin_ref[...].astype(BF16)

    @pl.when(pl.program_id(1) == 0)
    def _():
        ext_s[0:8, :] = jnp.zeros((8, ext_s.shape[1]), F32)
        c_s[...] = jnp.zeros(c_s.shape, F32)
        m_s[...] = jnp.zeros(m_s.shape, F32)
        g_s[...] = jnp.zeros(g_s.shape, F32)

    u = qk_ref[...].astype(F32)
    ext_s[8:8 + L, :] = u
    cw = cw_ref[...]
    conv = (cb_ref[...] + cw[3:4] * u + cw[2:3] * ext_s[7:7 + L, :]
            + cw[1:2] * ext_s[6:6 + L, :] + cw[0:1] * ext_s[5:5 + L, :])
    ext_s[0:8, :] = ext_s[L:L + 8, :]
    qk_s[...] = conv * _sigmoid(conv)

    zr = zr_ref[...]
    i_rows = zr[0:H]
    b_rows = _cumsum_lanes(_log_sigmoid(zr[H:2 * H]))
    g_s[0:H, :] = b_rows
    b_cols = g_s[...].T

    row = lax.broadcasted_iota(I32, (L, L), 0)
    col = lax.broadcasted_iota(I32, (L, L), 1)
    causal = row >= col
    ones_blk = jnp.where(lax.broadcasted_iota(I32, (L, LANES), 1) == 0, 1.0, 0.0).astype(BF16)

    for h in range(H):
        q = qk_s[:, h * DK:(h + 1) * DK]
        k = qk_s[:, (H + h) * DK:(H + h + 1) * DK] * (DK ** -0.5)
        kt = k.T
        b_col = b_cols[:, h:h + 1]
        b_row = b_rows[h:h + 1, :]
        i_row = i_rows[h:h + 1, :]
        m_prev = m_s[h:h + 1, 0:1]
        dmat = jnp.where(causal, b_col - b_row + i_row, -jnp.inf)
        m_inter = b_col + m_prev
        m_j = jnp.maximum(m_inter, jnp.max(dmat, axis=1, keepdims=True))
        q16 = q.astype(BF16)
        s = _dot(q16, kt.astype(BF16)) * jnp.exp(dmat - m_j)
        inter = jnp.exp(m_inter - m_j)
        vx = jnp.concatenate([v_ref[:, h * DV:(h + 1) * DV], ones_blk], axis=1)
        cx = c_s[h]
        nd = _dot(s.astype(BF16), vx) + inter * _dot(q16, cx.astype(BF16))
        den = nd[:, DV:DV + 1]
        hh = nd[:, 0:DV] / jnp.maximum(jnp.abs(den), jnp.exp(-m_j))
        b_last = b_row[:, L - 1:L]
        w_log = b_last - b_row + i_row
        m_new = jnp.maximum(b_last + m_prev, jnp.max(w_log, axis=1, keepdims=True))
        ktw = (kt * jnp.exp(w_log - m_new)).astype(BF16)
        c_s[h] = jnp.exp(b_last + m_prev - m_new) * cx + _dot(ktw, vx)
        m_s[h:h + 1, :] = jnp.broadcast_to(m_new, (1, LANES))
        yn = _rms(hh, gmh_ref[:, h * DV:(h + 1) * DV])
        y_ref[:, h * DV:(h + 1) * DV] = (yn * o_ref[:, h * DV:(h + 1) * DV].astype(F32)).astype(y_ref.dtype)


def _mlstm(z_main, zif_t, conv_w, conv_b, g_mh, cast_src, batch, seq, d):
    L, H = MLSTM_CHUNK, M_HEADS
    DV = d // H
    DK = DV // 2
    nc = seq // L
    n = batch * seq
    row_blk = lambda b, c: b * nc + c
    cast_rows, cast_cols = cast_src.shape
    cb, ncb = _cast_blocks(cast_rows, batch * nc)
    cast_map = lambda b, c: (jnp.minimum(row_blk(b, c), ncb - 1), 0)
    return pl.pallas_call(
        functools.partial(_mlstm_kernel, L=L, H=H, DK=DK, DV=DV),
        out_shape=(jax.ShapeDtypeStruct((n, d), BF16), jax.ShapeDtypeStruct(cast_src.shape, BF16)),
        grid=(batch, nc),
        in_specs=[
            pl.BlockSpec((L, d), lambda b, c: (row_blk(b, c), 0)),
            pl.BlockSpec((L, d), lambda b, c: (row_blk(b, c), 1)),
            pl.BlockSpec((L, d), lambda b, c: (row_blk(b, c), 2)),
            pl.BlockSpec((2 * H, L), lambda b, c: (0, row_blk(b, c))),
            pl.BlockSpec((CONV_W, d), lambda b, c: (0, 0)),
            pl.BlockSpec((1, d), lambda b, c: (0, 0)),
            pl.BlockSpec((1, d), lambda b, c: (0, 0)),
            pl.BlockSpec((cb, cast_cols), cast_map),
        ],
        out_specs=(pl.BlockSpec((L, d), lambda b, c: (row_blk(b, c), 0)),
                   pl.BlockSpec((cb, cast_cols), cast_map)),
        scratch_shapes=[
            pltpu.VMEM((L + 8, d), F32),
            pltpu.VMEM((L, d), F32),
            pltpu.VMEM((H, DK, DV + LANES), F32),
            pltpu.VMEM((H, LANES), F32),
            pltpu.VMEM((LANES, L), F32),
        ],
        compiler_params=_cparams(("arbitrary", "arbitrary")),
        name="mlstm",
    )(z_main, z_main, z_main, zif_t, conv_w, conv_b, g_mh, cast_src)


def _sgu_kernel(u_ref, v_ref, ws_ref, bst_ref, g_ref, y_ref, vn_s, *, R, DG):
    vn_s[...] = _rms(v_ref[...].astype(F32), g_ref[...]).astype(BF16)
    row = lax.broadcasted_iota(I32, (G_CHUNK, G_CHUNK), 0)
    col = lax.broadcasted_iota(I32, (G_CHUNK, G_CHUNK), 1)
    for g in range(G_GROUPS):
        ws = jnp.where(row >= col, ws_ref[g], 0.0).astype(BF16)
        bias = bst_ref[:, g:g + 1]
        for c in range(R // G_CHUNK):
            rs = slice(c * G_CHUNK, (c + 1) * G_CHUNK)
            cs = slice(g * DG, (g + 1) * DG)
            vm = _dot(ws, vn_s[rs, cs]) + bias
            y_ref[rs, cs] = (u_ref[rs, cs].astype(F32) * vm).astype(y_ref.dtype)


def _sgu(z_main, w_s, b_s_t, g_sgu, d, R):
    n = z_main.shape[0]
    return pl.pallas_call(
        functools.partial(_sgu_kernel, R=R, DG=d // G_GROUPS),
        out_shape=jax.ShapeDtypeStruct((n, d), BF16),
        grid=(n // R,),
        in_specs=[
            pl.BlockSpec((R, d), lambda i: (i, 3)),
            pl.BlockSpec((R, d), lambda i: (i, 4)),
            pl.BlockSpec((G_GROUPS, G_CHUNK, G_CHUNK), lambda i: (0, 0, 0)),
            pl.BlockSpec((G_CHUNK, G_GROUPS), lambda i: (0, 0)),
            pl.BlockSpec((1, d), lambda i: (0, 0)),
        ],
        out_specs=pl.BlockSpec((R, d), lambda i: (i, 0)),
        scratch_shapes=[pltpu.VMEM((R, d), BF16)],
        compiler_params=_cparams(("arbitrary",)),
        name="spatial_gating",
    )(z_main, z_main, w_s, b_s_t, g_sgu)


def _merge_kernel(ya_ref, yb_ref, ga_ref, gb_ref, w0_ref, w1_ref, o_ref):
    a = _dot(ya_ref[...], w0_ref[...])
    b = _dot(yb_ref[...], w1_ref[...])
    o_ref[...] = (ga_ref[...].astype(F32) * a + gb_ref[...].astype(F32) * b).astype(o_ref.dtype)


def _merge(y_a, y_b, z_main, w_br, d, tm, tn):
    n = y_a.shape[0]
    ga0 = 5 * d // tn
    gb0 = 6 * d // tn
    return pl.pallas_call(
        _merge_kernel,
        out_shape=jax.ShapeDtypeStruct((n, d), BF16),
        grid=(d // tn, n // tm),
        in_specs=[
            pl.BlockSpec((tm, d), lambda j, i: (i, 0)),
            pl.BlockSpec((tm, d), lambda j, i: (i, 0)),
            pl.BlockSpec((tm, tn), lambda j, i: (i, ga0 + j)),
            pl.BlockSpec((tm, tn), lambda j, i: (i, gb0 + j)),
            pl.BlockSpec((None, d, tn), lambda j, i: (0, 0, j)),
            pl.BlockSpec((None, d, tn), lambda j, i: (1, 0, j)),
        ],
        out_specs=pl.BlockSpec((tm, tn), lambda j, i: (i, j)),
        compiler_params=_cparams(("arbitrary", "arbitrary")),
        name="branch_merge",
    )(y_a, y_b, z_main, z_main, w_br, w_br)


def _split_bf16(x):
    hi = x.astype(BF16)
    return hi, (x - hi.astype(F32)).astype(BF16)


def _outproj_kernel(x_ref, mix_ref, wo_ref, gf_ref, wr_ref, br_ref,
                    x1_ref, hp_ref, idx_ref, gate_ref, cnt_ref, cnt_s, *, E):
    @pl.when(pl.program_id(0) == 0)
    def _():
        cnt_s[...] = jnp.zeros(cnt_s.shape, F32)

    x1 = x_ref[...] + _dot(mix_ref[...], wo_ref[...])
    x1_ref[...] = x1
    hn = _rms(x1, gf_ref[...])
    hp_ref[...] = hn
    h_hi, h_lo = _split_bf16(hn)
    w_hi, w_lo = _split_bf16(wr_ref[...])
    logits = _dot(h_hi, w_hi) + (_dot(h_hi, w_lo) + _dot(h_lo, w_hi)) + br_ref[...]
    lane = lax.broadcasted_iota(I32, logits.shape, 1)
    work = jnp.where(lane < E, logits, -jnp.inf)
    vals, idxs = [], []
    for _ in range(TOP_K):
        mx = jnp.max(work, axis=1, keepdims=True)
        ix = jnp.min(jnp.where(work == mx, lane, LANES), axis=1, keepdims=True)
        vals.append(mx)
        idxs.append(ix)
        work = jnp.where(lane == ix, -jnp.inf, work)
    exps = [jnp.exp(v - vals[0]) for v in vals]
    inv = 1.0 / functools.reduce(jnp.add, exps)
    tm = logits.shape[0]
    onehots = [jnp.where(lane == ix, 1.0, 0.0) for ix in idxs]
    oh_sum = functools.reduce(jnp.add, onehots)
    earlier = lax.broadcasted_iota(I32, (tm, tm), 0) > lax.broadcasted_iota(I32, (tm, tm), 1)
    before = _dot(jnp.where(earlier, 1.0, 0.0).astype(BF16), oh_sum.astype(BF16)) + cnt_s[0:1, :]
    cnt = cnt_s[0:1, :] + jnp.sum(oh_sum, axis=0, keepdims=True)
    cnt_s[...] = jnp.broadcast_to(cnt, cnt_s.shape)
    cnt_ref[...] = jnp.broadcast_to(cnt, cnt_s.shape).astype(I32)
    idx_out = jnp.zeros(logits.shape, I32)
    gate_out = jnp.zeros(logits.shape, F32)
    for k in range(TOP_K):
        rank = jnp.sum(onehots[k] * before, axis=1, keepdims=True).astype(I32)
        idx_out = jnp.where(lane == k, idxs[k], idx_out)
        idx_out = jnp.where(lane == TOP_K + k, rank, idx_out)
        gate_out = jnp.where(lane == k, exps[k] * inv, gate_out)
    idx_ref[...] = idx_out
    gate_ref[...] = gate_out


def _out_proj(x2, mix, w_out, g_ffn, w_router, b_router, E, tm):
    n, d = x2.shape
    return pl.pallas_call(
        functools.partial(_outproj_kernel, E=E),
        out_shape=(
            jax.ShapeDtypeStruct((n, d), F32),
            jax.ShapeDtypeStruct((n, d), F32),
            jax.ShapeDtypeStruct((n, LANES), I32),
            jax.ShapeDtypeStruct((n, LANES), F32),
            jax.ShapeDtypeStruct((8, LANES), I32),
        ),
        grid=(n // tm,),
        in_specs=[
            pl.BlockSpec((tm, d), lambda i: (i, 0)),
            pl.BlockSpec((tm, d), lambda i: (i, 0)),
            pl.BlockSpec((d, d), lambda i: (0, 0)),
            pl.BlockSpec((1, d), lambda i: (0, 0)),
            pl.BlockSpec((d, LANES), lambda i: (0, 0)),
            pl.BlockSpec((1, LANES), lambda i: (0, 0)),
        ],
        out_specs=(
            pl.BlockSpec((tm, d), lambda i: (i, 0)),
            pl.BlockSpec((tm, d), lambda i: (i, 0)),
            pl.BlockSpec((tm, LANES), lambda i: (i, 0)),
            pl.BlockSpec((tm, LANES), lambda i: (i, 0)),
            pl.BlockSpec((8, LANES), lambda i: (0, 0)),
        ),
        scratch_shapes=[pltpu.VMEM((8, LANES), F32)],
        compiler_params=_cparams(("arbitrary",)),
        name="out_proj_router",
    )(x2, mix, w_out, g_ffn, w_router, b_router)


def _route_plan(top_idx, rank, counts, E, P):
    n, k = top_idx.shape
    a = n * k
    e_flat = top_idx.reshape(a)
    padded = ((counts + MOE_SUB - 1) // MOE_SUB) * MOE_SUB
    pend = jnp.cumsum(padded)
    pstart = pend - padded
    onehot = e_flat[:, None] == jnp.arange(E, dtype=I32)[None, :]
    dest = (rank.reshape(a) + jnp.sum(jnp.where(onehot, pstart[None, :], 0), axis=1)).astype(I32)

    nsb = P // MOE_SUB
    spw = MOE_WIN // MOE_SUB
    ni = P // MOE_WIN + E
    sb = jnp.arange(nsb, dtype=I32)
    sb_e = jnp.sum((pend[None, :] <= (sb * MOE_SUB)[:, None]).astype(I32), axis=1)
    sb_e = jnp.minimum(sb_e, E - 1)
    valid = sb * MOE_SUB < pend[-1]
    sb_e = jnp.where(valid, sb_e, sb_e[pend[-1] // MOE_SUB - 1])
    prev_e = jnp.concatenate([jnp.full((1,), -1, I32), sb_e[:-1]])
    is_start = (sb % spw == 0) | (sb_e != prev_e)
    item_of_sb = jnp.cumsum(is_start.astype(I32)) - 1
    n_items = jnp.sum(is_start.astype(I32))
    slot = jnp.where(is_start, item_of_sb, ni)
    item_win = jnp.zeros((ni,), I32).at[slot].set(sb // spw, mode="drop")
    item_e = jnp.zeros((ni,), I32).at[slot].set(sb_e, mode="drop")
    item_lo = jnp.zeros((ni,), I32).at[slot].set(sb % spw, mode="drop")
    win_has_unused = jnp.zeros((nsb // spw,), I32).at[sb // spw].add(1 - valid.astype(I32)) > 0
    zero_win = (sb % spw == 0) & win_has_unused[sb // spw]
    item_first = jnp.zeros((ni,), I32).at[slot].set(zero_win.astype(I32), mode="drop")
    item_len = jnp.zeros((ni,), I32).at[jnp.where(valid, item_of_sb, ni)].add(1, mode="drop")
    it = jnp.arange(ni, dtype=I32)
    live = it < n_items
    last = jnp.maximum(n_items - 1, 0)
    src = jnp.minimum(it, last)
    item_win = item_win[src]
    item_e = item_e[src]
    item_lo = jnp.where(live, item_lo, 0)
    item_hi = jnp.where(live, item_lo + item_len, 0)
    item_first = jnp.where(live, item_first, 0)
    padinfo = jnp.concatenate([pstart + counts, padded - counts, pend[-1:] // MOE_SUB]).astype(I32)
    return dest, padinfo, (item_win, item_e, item_lo, item_hi, item_first)


def _dispatch_kernel(dest_ref, pad_ref, hp_ref, xs_ref, sem, pad_sem, tail_sem, *, G, E):
    base = pl.program_id(0) * G

    def row_copy(r, d, s):
        return pltpu.make_async_copy(hp_ref.at[pl.ds(r, 1)], xs_ref.at[pl.ds(d, 1)], s)

    def issue(r, carry):
        for k in range(TOP_K):
            row_copy(r, dest_ref[(base + r) * TOP_K + k], sem).start(priority=k % 2)
        return carry

    lax.fori_loop(0, G, issue, 0)

    @pl.when(pl.program_id(0) == 0)
    def _():
        def fill(e, carry):
            start = pad_ref[e]

            def one(r, c):
                row_copy(0, start + r, pad_sem).start()
                return c

            return lax.fori_loop(0, pad_ref[E + e], one, carry)

        def drain(e, carry):
            def one(r, c):
                row_copy(0, 0, pad_sem).wait()
                return c

            return lax.fori_loop(0, pad_ref[E + e], one, carry)

        def tail_copy(s):
            rows = pl.ds(pl.multiple_of(s * MOE_SUB, MOE_SUB), MOE_SUB)
            return pltpu.make_async_copy(hp_ref.at[pl.ds(0, MOE_SUB)], xs_ref.at[rows], tail_sem)

        def tail_fill(s, carry):
            tail_copy(s).start()
            return carry

        def tail_drain(s, carry):
            tail_copy(s).wait()
            return carry

        n_sub = xs_ref.shape[0] // MOE_SUB
        lax.fori_loop(0, E, fill, 0)
        lax.fori_loop(pad_ref[2 * E], n_sub, tail_fill, 0)
        lax.fori_loop(0, E, drain, 0)
        lax.fori_loop(pad_ref[2 * E], n_sub, tail_drain, 0)

    pltpu.make_async_copy(xs_ref.at[pl.ds(0, G * TOP_K)], xs_ref.at[pl.ds(0, G * TOP_K)], sem).wait()


def _dispatch(dest, padinfo, hp, P, G, E):
    n, w = hp.shape
    grid_spec = pltpu.PrefetchScalarGridSpec(
        num_scalar_prefetch=2,
        grid=(n // G,),
        in_specs=[pl.BlockSpec((G, w), lambda i, dest, pad: (i, 0))],
        out_specs=pl.BlockSpec(memory_space=pl.ANY),
        scratch_shapes=[pltpu.SemaphoreType.DMA] * 3,
    )
    return pl.pallas_call(
        functools.partial(_dispatch_kernel, G=G, E=E),
        out_shape=jax.ShapeDtypeStruct((P, w), F32),
        grid_spec=grid_spec,
        compiler_params=_cparams(("arbitrary",)),
        name="moe_dispatch",
    )(dest, padinfo, hp)


def _moe_kernel(win_ref, e_ref, lo_ref, hi_ref, first_ref,
                xs_ref, wg_ref, wl_ref, wd_ref, bg_ref, bl_ref, bd_ref, y_ref,
                stage_s, xb_s, sem, *, NI, NJ):
    del e_ref
    w = pl.program_id(0)
    j = pl.program_id(1)
    lo = lo_ref[w]
    hi = hi_ref[w]
    spw = MOE_WIN // MOE_SUB

    def sub_rows(s):
        return pl.ds(pl.multiple_of(s * MOE_SUB, MOE_SUB), MOE_SUB)

    def x_copy(item, s):
        src = pl.ds(pl.multiple_of((win_ref[item] * spw + s) * MOE_SUB, MOE_SUB), MOE_SUB)
        return pltpu.make_async_copy(xs_ref.at[src], stage_s.at[sub_rows(s)], sem)

    def start_item(item):
        def body(s, carry):
            x_copy(item, s).start()
            return carry

        lax.fori_loop(lo_ref[item], hi_ref[item], body, 0)

    @pl.when(j == 0)
    def _():
        @pl.when(w == 0)
        def _():
            start_item(0)

        @pl.when(first_ref[w] == 1)
        def _():
            y_ref[...] = jnp.zeros(y_ref.shape, F32)

        def wait_one(s, carry):
            x_copy(w, s).wait()
            return carry

        lax.fori_loop(lo, hi, wait_one, 0)

    @pl.when(jnp.logical_and(j == 1, w + 1 < NI))
    def _():
        start_item(w + 1)

    def mlp(r0, nrows, first_tile):
        rows = pl.ds(r0, nrows)
        if first_tile:
            xb = stage_s[rows, :].astype(BF16)
            xb_s[rows, :] = xb
        else:
            xb = xb_s[rows, :]
        hg = jnp.minimum(_dot(xb, wg_ref[...]) + bg_ref[...], SWIGLU_LIMIT)
        hl = jnp.clip(_dot(xb, wl_ref[...]) + bl_ref[...], -SWIGLU_LIMIT, SWIGLU_LIMIT)
        act = (hg * _sigmoid(SWIGLU_ALPHA * hg) * (hl + 1.0)).astype(BF16)
        out = _dot(act, wd_ref[...])
        if first_tile:
            y_ref[rows, :] = out + bd_ref[...]
        else:
            y_ref[rows, :] += out

    for g in range(1, spw + 1):
        for first_tile in (True, False):
            @pl.when(jnp.logical_and(hi - lo == g, (j == 0) if first_tile else (j > 0)))
            def _(g=g, first_tile=first_tile):
                done = 0
                while done < g:
                    step = min(2, g - done)
                    mlp(pl.multiple_of((lo + done) * MOE_SUB, MOE_SUB), step * MOE_SUB, first_tile)
                    done += step


def _moe(plan, xs, w_gu, b_gu, w_dn, b_dn):
    P, d = xs.shape
    E, _, f2 = w_gu.shape
    f = f2 // 2
    nj = f // MOE_TF
    assert nj >= 2, "the next item's rows are prefetched at the second hidden tile"
    ni = plan[0].shape[0]
    grid_spec = pltpu.PrefetchScalarGridSpec(
        num_scalar_prefetch=5,
        grid=(ni, nj),
        in_specs=[
            pl.BlockSpec(memory_space=pl.ANY),
            pl.BlockSpec((None, d, MOE_TF), lambda w, j, win, e, lo, hi, fi: (e[w], 0, j)),
            pl.BlockSpec((None, d, MOE_TF), lambda w, j, win, e, lo, hi, fi: (e[w], 0, nj + j)),
            pl.BlockSpec((None, MOE_TF, d), lambda w, j, win, e, lo, hi, fi: (e[w], j, 0)),
            pl.BlockSpec((None, 1, MOE_TF), lambda w, j, win, e, lo, hi, fi: (e[w], 0, j)),
            pl.BlockSpec((None, 1, MOE_TF), lambda w, j, win, e, lo, hi, fi: (e[w], 0, nj + j)),
            pl.BlockSpec((None, 1, d), lambda w, j, win, e, lo, hi, fi: (e[w], 0, 0)),
        ],
        out_specs=pl.BlockSpec((MOE_WIN, d), lambda w, j, win, e, lo, hi, fi: (win[w], 0)),
        scratch_shapes=[
            pltpu.VMEM((MOE_WIN, d), F32),
            pltpu.VMEM((MOE_WIN, d), BF16),
            pltpu.SemaphoreType.DMA,
        ],
    )
    return pl.pallas_call(
        functools.partial(_moe_kernel, NI=ni, NJ=nj),
        out_shape=jax.ShapeDtypeStruct((P, d), F32),
        grid_spec=grid_spec,
        compiler_params=_cparams(("arbitrary", "arbitrary")),
        name="moe_experts",
    )(*plan, xs, w_gu, w_gu, w_dn, b_gu, b_gu, b_dn)


def _final_kernel(dest_ref, x1_ref, gate_ref, p_ref, wple_ref, wpg_ref, gpost_ref, gple_ref, gfin_ref,
                  y_ref, o_ref, yg_s, x3_s, emb_s, xg_s, sem, *, TM, NT, NC, final_norm):
    i = pl.program_id(0)
    slot = i % 2
    nslot = 1 - slot
    nxt = jnp.minimum(i + 1, NT - 1)

    def row_copy(tile, r, k, to_slot):
        d = dest_ref[(tile * TM + r) * TOP_K + k]
        return pltpu.make_async_copy(y_ref.at[pl.ds(d, 1)], yg_s.at[to_slot, k, pl.ds(r, 1)], sem.at[to_slot])

    def wait_slot(s):
        for k in range(TOP_K):
            pltpu.make_async_copy(y_ref.at[pl.ds(0, TM)], yg_s.at[s, k], sem.at[s]).wait()

    @pl.when(i == 0)
    def _():
        def issue(r, carry):
            for k in range(TOP_K):
                row_copy(0, r, k, 0).start(priority=k % 2)
            return carry

        lax.fori_loop(0, TM, issue, 0)

    wait_slot(slot)
    x2 = x1_ref[...]
    for k in range(TOP_K):
        x2 = x2 + gate_ref[:, k:k + 1] * yg_s[slot, k]
    x3_s[...] = x2
    xg_s[...] = _rms(x2, gple_ref[...]).astype(BF16)
    emb_s[...] = _rms(_dot(p_ref[...].astype(BF16), wple_ref[...]), gpost_ref[...])
    cw = x3_s.shape[1] // NC
    per = TM // NC
    for c in range(NC):
        cols = slice(c * cw, (c + 1) * cw)
        pg = _sigmoid(_dot(xg_s[...], wpg_ref[:, cols]))
        x3_s[:, cols] += pg * emb_s[:, cols]
        for r in range(c * per, (c + 1) * per):
            for k in range(TOP_K):
                row_copy(nxt, r, k, nslot).start(priority=k % 2)
    x3 = x3_s[...]
    if final_norm:
        x3 = _rms(x3, gfin_ref[...])
    o_ref[...] = x3

    @pl.when(i == NT - 1)
    def _():
        wait_slot(nslot)


def _final(dest, x1, gates, p2, w_ple, w_pg, g_post, g_ple, g_fin, y, tm, final_norm):
    n, d = x1.shape
    pd = p2.shape[1]
    const = lambda i, dest: (0, 0)
    grid_spec = pltpu.PrefetchScalarGridSpec(
        num_scalar_prefetch=1,
        grid=(n // tm,),
        in_specs=[
            pl.BlockSpec((tm, d), lambda i, dest: (i, 0)),
            pl.BlockSpec((tm, LANES), lambda i, dest: (i, 0)),
            pl.BlockSpec((tm, pd), lambda i, dest: (i, 0)),
            pl.BlockSpec((pd, d), const),
            pl.BlockSpec((d, d), const),
            pl.BlockSpec((1, d), const),
            pl.BlockSpec((1, d), const),
            pl.BlockSpec((1, d), const),
            pl.BlockSpec(memory_space=pl.ANY),
        ],
        out_specs=pl.BlockSpec((tm, d), lambda i, dest: (i, 0)),
        scratch_shapes=[
            pltpu.VMEM((2, TOP_K, tm, d), F32),
            pltpu.VMEM((tm, d), F32),
            pltpu.VMEM((tm, d), F32),
            pltpu.VMEM((tm, d), BF16),
            pltpu.SemaphoreType.DMA((2,)),
        ],
    )
    return pl.pallas_call(
        functools.partial(_final_kernel, TM=tm, NT=n // tm, NC=8, final_norm=final_norm),
        out_shape=jax.ShapeDtypeStruct((n, d), F32),
        grid_spec=grid_spec,
        compiler_params=_cparams(("arbitrary",)),
        name="combine_ple",
    )(dest, x1, gates, p2, w_ple, w_pg, g_post, g_ple, g_fin, y)


def _layer(x2, p2, batch, seq, g_mix, w_in, conv_w, conv_b, b_if, g_mh, g_sgu, w_s, b_s, w_br, w_out,
           g_ffn, w_router, b_router, w_gu, b_gu, w_dn, b_dn, g_ple, w_pg, w_ple, g_ple_post, g_final,
           final_norm):
    n, d = x2.shape
    H = M_HEADS
    E = w_router.shape[1]
    row = lambda v: v.reshape(1, -1).astype(F32)

    c_if = 3 * d
    w_if = jnp.pad(w_in[:, c_if:c_if + 2 * H], ((0, 0), (0, LANES - 2 * H))).astype(BF16)
    bias_if = jnp.pad(b_if.reshape(1, 2 * H), ((0, 0), (0, LANES - 2 * H))).astype(F32)

    tm_a = min(512, n)
    xn, z_if = _input_norm(x2, row(g_mix), w_if, bias_if, tm_a)

    tm = min(1024, n)
    tn = 1024
    n_plain = 2 * d // tn
    sig_tiles = tuple(range(2 * d // tn, 3 * d // tn)) + tuple(range(5 * d // tn, 7 * d // tn))
    f2 = w_gu.shape[2]
    z_main, w_gu16 = _input_proj(xn, w_in.T, w_gu.reshape(E * d, f2), 7 * d, tm, tn, n_plain, c_if // tn,
                                 sig_tiles, 2 * H)

    zif_t = z_if[:, :2 * H].T
    y_a, w_dn16 = _mlstm(z_main, zif_t, conv_w.astype(F32), row(conv_b), row(g_mh),
                         w_dn.reshape(-1, d), batch, seq, d)
    y_b = _sgu(z_main, w_s.astype(F32), b_s.T.astype(F32), row(g_sgu), d, min(512, n))
    mix = _merge(y_a, y_b, z_main, w_br.astype(BF16), d, tm, 512)

    w_r = jnp.pad(w_router, ((0, 0), (0, LANES - E))).astype(F32)
    b_r = jnp.pad(b_router.reshape(1, E), ((0, 0), (0, LANES - E))).astype(F32)
    x1, hp, idx, gates, cnt = _out_proj(x2, mix, w_out.astype(BF16), row(g_ffn), w_r, b_r, E, min(256, n))

    P = n * TOP_K + E * MOE_SUB
    P = -(-P // MOE_WIN) * MOE_WIN
    dest, padinfo, plan = _route_plan(idx[:, :TOP_K], idx[:, TOP_K:2 * TOP_K], cnt[0, :E], E, P)
    xs = _dispatch(dest, padinfo, hp, P, min(1024, n), E)
    y = _moe(plan, xs, w_gu16.reshape(w_gu.shape), b_gu.reshape(E, 1, -1), w_dn16.reshape(w_dn.shape),
             b_dn.reshape(E, 1, -1))
    return _final(dest, x1, gates, p2, w_ple.astype(BF16), w_pg.astype(BF16), row(g_ple_post),
                  row(g_ple), row(g_final), y, min(256, n), final_norm)


def kernel(x, p, g_mix, w_in, conv_w, conv_b, b_if, g_mh, g_sgu, w_s, b_s, w_br, w_out, g_ffn, w_router,
           b_router, w_gu, b_gu, w_dn, b_dn, g_ple, w_pg, w_ple, g_ple_post, g_final):
    batch, seq, d = x.shape
    depth = p.shape[0]
    x2 = x.reshape(batch * seq, d)
    for i in range(depth):
        x2 = _layer(x2, p[i].reshape(batch * seq, -1), batch, seq, g_mix[i], w_in[i], conv_w[i], conv_b[i],
                    b_if[i], g_mh[i], g_sgu[i], w_s[i], b_s[i], w_br[i], w_out[i], g_ffn[i], w_router[i],
                    b_router[i], w_gu[i], b_gu[i], w_dn[i], b_dn[i], g_ple[i], w_pg[i], w_ple[i],
                    g_ple_post[i], g_final, final_norm=(i == depth - 1))
    return x2.reshape(batch, seq, d)
```

```python
import functools

import jax
import jax.numpy as jnp
from jax import lax
from jax.experimental import pallas as pl
from jax.experimental.pallas import tpu as pltpu

F32 = jnp.float32
BF16 = jnp.bfloat16
I32 = jnp.int32

RMS_EPS = 1e-6
M_HEADS = 8
CONV_W = 4
G_GROUPS = 8
G_CHUNK = 128
TOP_K = 4
SWIGLU_ALPHA = 1.702
SWIGLU_LIMIT = 7.0

LANES = 128
MLSTM_CHUNK = 256
MOE_SUB = 256
MOE_WIN = 1024
MOE_TF = 512
VMEM_LIMIT = 56 * 1024 * 1024


def _cparams(sem, vmem=VMEM_LIMIT):
    return pltpu.CompilerParams(dimension_semantics=sem, vmem_limit_bytes=vmem)


def _dot(a, b, **kw):
    return jnp.dot(a, b, preferred_element_type=F32, **kw)


def _rms(x, g):
    ms = jnp.mean(x * x, axis=-1, keepdims=True)
    return x * lax.rsqrt(ms + RMS_EPS) * g


def _sigmoid(x):
    return 0.5 * jnp.tanh(0.5 * x) + 0.5


def _norm_kernel(x_ref, g_ref, wif_ref, bif_ref, xn_ref, zif_ref):
    xn = _rms(x_ref[...], g_ref[...]).astype(BF16)
    xn_ref[...] = xn
    zif_ref[...] = _dot(xn, wif_ref[...]) + bif_ref[...]


def _input_norm(x2, g, w_if, b_if, tm):
    n, d = x2.shape
    return pl.pallas_call(
        _norm_kernel,
        out_shape=(jax.ShapeDtypeStruct((n, d), BF16), jax.ShapeDtypeStruct((n, LANES), F32)),
        grid=(n // tm,),
        in_specs=[
            pl.BlockSpec((tm, d), lambda i: (i, 0)),
            pl.BlockSpec((1, d), lambda i: (0, 0)),
            pl.BlockSpec((d, LANES), lambda i: (0, 0)),
            pl.BlockSpec((1, LANES), lambda i: (0, 0)),
        ],
        out_specs=(pl.BlockSpec((tm, d), lambda i: (i, 0)), pl.BlockSpec((tm, LANES), lambda i: (i, 0))),
        compiler_params=_cparams(("arbitrary",)),
        name="input_norm",
    )(x2, g, w_if, b_if)


def _inproj_kernel(x_ref, wa_ref, wb_ref, cin_ref, o_ref, cout_ref, w_s, *,
                   n_plain, n_aligned, sig_tiles, shift, chunks):
    j = pl.program_id(0)

    @pl.when(pl.program_id(1) == 0)
    def _():
        @pl.when(j < n_aligned)
        def _():
            w_s[...] = wa_ref[...].T.astype(BF16)

        @pl.when(j >= n_aligned)
        def _():
            w_s[...] = jnp.concatenate([wa_ref[shift:, :], wb_ref[...]], axis=0).T.astype(BF16)

    cm = x_ref.shape[0] // chunks

    cc = cin_ref.shape[0] // chunks

    def body(epilogue):
        for r in range(chunks):
            rows = slice(r * cm, (r + 1) * cm)
            o_ref[rows, :] = epilogue(_dot(x_ref[rows, :], w_s[...])).astype(o_ref.dtype)
            crows = slice(r * cc, (r + 1) * cc)
            cout_ref[crows, :] = cin_ref[crows, :].astype(BF16)

    is_sig = functools.reduce(jnp.logical_or, [j == t for t in sig_tiles])
    is_plain = j < n_plain
    pl.when(is_plain)(lambda: body(lambda a: a))
    pl.when(is_sig)(lambda: body(_sigmoid))
    pl.when(jnp.logical_not(jnp.logical_or(is_plain, is_sig)))(lambda: body(jax.nn.gelu))


def _cast_blocks(rows, steps, multiple=16):
    cb = -(-rows // steps)
    cb = -(-cb // multiple) * multiple
    return cb, -(-rows // cb)


def _input_proj(xn, w_in_t, cast_src, c_out, tm, tn, n_plain, n_aligned, sig_tiles, shift):
    n, d = xn.shape
    ni = n // tm
    cast_rows, cast_cols = cast_src.shape
    chunks = max(1, tm // 256)
    cb, ncb = _cast_blocks(cast_rows, (c_out // tn) * ni, 16 * chunks)
    cast_map = lambda j, i: (jnp.minimum(j * ni + i, ncb - 1), 0)
    b_per_tile = tn // shift
    return pl.pallas_call(
        functools.partial(_inproj_kernel, n_plain=n_plain, n_aligned=n_aligned, sig_tiles=sig_tiles,
                          shift=shift, chunks=chunks),
        out_shape=(jax.ShapeDtypeStruct((n, c_out), BF16), jax.ShapeDtypeStruct(cast_src.shape, BF16)),
        grid=(c_out // tn, ni),
        in_specs=[
            pl.BlockSpec((tm, d), lambda j, i: (i, 0)),
            pl.BlockSpec((tn, d), lambda j, i: (j, 0)),
            pl.BlockSpec((shift, d), lambda j, i: (jnp.maximum(j, n_aligned) * b_per_tile + b_per_tile, 0)),
            pl.BlockSpec((cb, cast_cols), cast_map),
        ],
        out_specs=(pl.BlockSpec((tm, tn), lambda j, i: (i, j)), pl.BlockSpec((cb, cast_cols), cast_map)),
        scratch_shapes=[pltpu.VMEM((d, tn), BF16)],
        compiler_params=_cparams(("arbitrary", "arbitrary")),
        name="input_proj",
    )(xn, w_in_t, w_in_t, cast_src)


def _log_sigmoid(x):
    return jnp.minimum(x, 0.0) - jnp.log1p(jnp.exp(-jnp.abs(x)))


def _cumsum_lanes(x):
    n = x.shape[1]
    lane = lax.broadcasted_iota(I32, x.shape, 1)
    sh = 1
    while sh < n:
        x = x + jnp.where(lane >= sh, pltpu.roll(x, sh, axis=1), 0.0)
        sh *= 2
    return x


def _mlstm_kernel(qk_ref, v_ref, o_ref, zr_ref, cw_ref, cb_ref, gmh_ref, cin_ref, y_ref, cout_ref,
                  ext_s, qk_s, c_s, m_s, g_s, *, L, H, DK, DV):
    @pl.when(pl.program_id(1) == 0)
    def _():
        ext_s[0:8, :] = jnp.zeros((8, ext_s.shape[1]), F32)
        c_s[...] = jnp.zeros(c_s.shape, F32)
        m_s[...] = jnp.zeros(m_s.shape, F32)
        g_s[...] = jnp.zeros(g_s.shape, F32)

    u = qk_ref[...].astype(F32)
    ext_s[8:8 + L, :] = u
    cw = cw_ref[...]
    conv = (cb_ref[...] + cw[3:4] * u + cw[2:3] * ext_s[7:7 + L, :]
            + cw[1:2] * ext_s[6:6 + L, :] + cw[0:1] * ext_s[5:5 + L, :])
    ext_s[0:8, :] = ext_s[L:L + 8, :]
    qk_s[...] = conv * _sigmoid(conv)

    zr = zr_ref[...]
    i_rows = zr[0:H]
    b_rows = _cumsum_lanes(_log_sigmoid(zr[H:2 * H]))
    g_s[0:H, :] = b_rows
    b_cols = g_s[...].T

    row = lax.broadcasted_iota(I32, (L, L), 0)
    col = lax.broadcasted_iota(I32, (L, L), 1)
    causal = row >= col
    ones_blk = jnp.where(lax.broadcasted_iota(I32, (L, LANES), 1) == 0, 1.0, 0.0).astype(BF16)
    cc = cin_ref.shape[0] // H

    for h in range(H):
        q = qk_s[:, h * DK:(h + 1) * DK]
        k = qk_s[:, (H + h) * DK:(H + h + 1) * DK] * (DK ** -0.5)
        kt = k.T
        b_col = b_cols[:, h:h + 1]
        b_row = b_rows[h:h + 1, :]
        i_row = i_rows[h:h + 1, :]
        m_prev = m_s[h:h + 1, 0:1]
        dmat = jnp.where(causal, b_col - b_row + i_row, -jnp.inf)
        m_inter = b_col + m_prev
        m_j = jnp.maximum(m_inter, jnp.max(dmat, axis=1, keepdims=True))
        q16 = q.astype(BF16)
        s = _dot(q16, kt.astype(BF16)) * jnp.exp(dmat - m_j)
        inter = jnp.exp(m_inter - m_j)
        vx = jnp.concatenate([v_ref[:, h * DV:(h + 1) * DV], ones_blk], axis=1)
        cx = c_s[h]
        nd = _dot(s.astype(BF16), vx) + inter * _dot(q16, cx.astype(BF16))
        den = nd[:, DV:DV + 1]
        hh = nd[:, 0:DV] / jnp.maximum(jnp.abs(den), jnp.exp(-m_j))
        b_last = b_row[:, L - 1:L]
        w_log = b_last - b_row + i_row
        m_new = jnp.maximum(b_last + m_prev, jnp.max(w_log, axis=1, keepdims=True))
        ktw = (kt * jnp.exp(w_log - m_new)).astype(BF16)
        c_s[h] = jnp.exp(b_last + m_prev - m_new) * cx + _dot(ktw, vx)
        m_s[h:h + 1, :] = jnp.broadcast_to(m_new, (1, LANES))
        yn = _rms(hh, gmh_ref[:, h * DV:(h + 1) * DV])
        y_ref[:, h * DV:(h + 1) * DV] = (yn * o_ref[:, h * DV:(h + 1) * DV].astype(F32)).astype(y_ref.dtype)
        crows = slice(h * cc, (h + 1) * cc)
        cout_ref[crows, :] = cin_ref[crows, :].astype(BF16)


def _mlstm(z_main, zif_t, conv_w, conv_b, g_mh, cast_src, batch, seq, d):
    L, H = MLSTM_CHUNK, M_HEADS
    DV = d // H
    DK = DV // 2
    nc = seq // L
    n = batch * seq
    row_blk = lambda b, c: b * nc + c
    cast_rows, cast_cols = cast_src.shape
    cb, ncb = _cast_blocks(cast_rows, batch * nc, 16 * H)
    cast_map = lambda b, c: (jnp.minimum(row_blk(b, c), ncb - 1), 0)
    return pl.pallas_call(
        functools.partial(_mlstm_kernel, L=L, H=H, DK=DK, DV=DV),
        out_shape=(jax.ShapeDtypeStruct((n, d), BF16), jax.ShapeDtypeStruct(cast_src.shape, BF16)),
        grid=(batch, nc),
        in_specs=[
            pl.BlockSpec((L, d), lambda b, c: (row_blk(b, c), 0)),
            pl.BlockSpec((L, d), lambda b, c: (row_blk(b, c), 1)),
            pl.BlockSpec((L, d), lambda b, c: (row_blk(b, c), 2)),
            pl.BlockSpec((2 * H, L), lambda b, c: (0, row_blk(b, c))),
            pl.BlockSpec((CONV_W, d), lambda b, c: (0, 0)),
            pl.BlockSpec((1, d), lambda b, c: (0, 0)),
            pl.BlockSpec((1, d), lambda b, c: (0, 0)),
            pl.BlockSpec((cb, cast_cols), cast_map),
        ],
        out_specs=(pl.BlockSpec((L, d), lambda b, c: (row_blk(b, c), 0)),
                   pl.BlockSpec((cb, cast_cols), cast_map)),
        scratch_shapes=[
            pltpu.VMEM((L + 8, d), F32),
            pltpu.VMEM((L, d), F32),
            pltpu.VMEM((H, DK, DV + LANES), F32),
            pltpu.VMEM((H, LANES), F32),
            pltpu.VMEM((LANES, L), F32),
        ],
        compiler_params=_cparams(("arbitrary", "arbitrary")),
        name="mlstm",
    )(z_main, z_main, z_main, zif_t, conv_w, conv_b, g_mh, cast_src)


def _sgu_kernel(u_ref, v_ref, ws_ref, bst_ref, g_ref, y_ref, vn_s, *, R, DG):
    vn_s[...] = _rms(v_ref[...].astype(F32), g_ref[...]).astype(BF16)
    row = lax.broadcasted_iota(I32, (G_CHUNK, G_CHUNK), 0)
    col = lax.broadcasted_iota(I32, (G_CHUNK, G_CHUNK), 1)
    for g in range(G_GROUPS):
        ws = jnp.where(row >= col, ws_ref[g], 0.0).astype(BF16)
        bias = bst_ref[:, g:g + 1]
        for c in range(R // G_CHUNK):
            rs = slice(c * G_CHUNK, (c + 1) * G_CHUNK)
            cs = slice(g * DG, (g + 1) * DG)
            vm = _dot(ws, vn_s[rs, cs]) + bias
            y_ref[rs, cs] = (u_ref[rs, cs].astype(F32) * vm).astype(y_ref.dtype)


def _sgu(z_main, w_s, b_s_t, g_sgu, d, R):
    n = z_main.shape[0]
    return pl.pallas_call(
        functools.partial(_sgu_kernel, R=R, DG=d // G_GROUPS),
        out_shape=jax.ShapeDtypeStruct((n, d), BF16),
        grid=(n // R,),
        in_specs=[
            pl.BlockSpec((R, d), lambda i: (i, 3)),
            pl.BlockSpec((R, d), lambda i: (i, 4)),
            pl.BlockSpec((G_GROUPS, G_CHUNK, G_CHUNK), lambda i: (0, 0, 0)),
            pl.BlockSpec((G_CHUNK, G_GROUPS), lambda i: (0, 0)),
            pl.BlockSpec((1, d), lambda i: (0, 0)),
        ],
        out_specs=pl.BlockSpec((R, d), lambda i: (i, 0)),
        scratch_shapes=[pltpu.VMEM((R, d), BF16)],
        compiler_params=_cparams(("arbitrary",)),
        name="spatial_gating",
    )(z_main, z_main, w_s, b_s_t, g_sgu)


def _merge_kernel(ya_ref, yb_ref, ga_ref, gb_ref, w0_ref, w1_ref, o_ref):
    a = _dot(ya_ref[...], w0_ref[...])
    b = _dot(yb_ref[...], w1_ref[...])
    o_ref[...] = (ga_ref[...].astype(F32) * a + gb_ref[...].astype(F32) * b).astype(o_ref.dtype)


def _merge(y_a, y_b, z_main, w_br, d, tm, tn):
    n = y_a.shape[0]
    ga0 = 5 * d // tn
    gb0 = 6 * d // tn
    return pl.pallas_call(
        _merge_kernel,
        out_shape=jax.ShapeDtypeStruct((n, d), BF16),
        grid=(d // tn, n // tm),
        in_specs=[
            pl.BlockSpec((tm, d), lambda j, i: (i, 0)),
            pl.BlockSpec((tm, d), lambda j, i: (i, 0)),
            pl.BlockSpec((tm, tn), lambda j, i: (i, ga0 + j)),
            pl.BlockSpec((tm, tn), lambda j, i: (i, gb0 + j)),
            pl.BlockSpec((None, d, tn), lambda j, i: (0, 0, j)),
            pl.BlockSpec((None, d, tn), lambda j, i: (1, 0, j)),
        ],
        out_specs=pl.BlockSpec((tm, tn), lambda j, i: (i, j)),
        compiler_params=_cparams(("arbitrary", "arbitrary")),
        name="branch_merge",
    )(y_a, y_b, z_main, z_main, w_br, w_br)


def _split_bf16(x):
    hi = x.astype(BF16)
    return hi, (x - hi.astype(F32)).astype(BF16)


def _outproj_kernel(x_ref, mix_ref, wo_ref, gf_ref, wr_ref, br_ref,
                    x1_ref, hp_ref, idx_ref, gate_ref, cnt_ref, cnt_s, *, E):
    @pl.when(pl.program_id(0) == 0)
    def _():
        cnt_s[...] = jnp.zeros(cnt_s.shape, F32)

    x1 = x_ref[...] + _dot(mix_ref[...], wo_ref[...])
    x1_ref[...] = x1
    hn = _rms(x1, gf_ref[...])
    hp_ref[...] = hn
    h_hi, h_lo = _split_bf16(hn)
    w_hi, w_lo = _split_bf16(wr_ref[...])
    logits = _dot(h_hi, w_hi) + (_dot(h_hi, w_lo) + _dot(h_lo, w_hi)) + br_ref[...]
    lane = lax.broadcasted_iota(I32, logits.shape, 1)
    work = jnp.where(lane < E, logits, -jnp.inf)
    vals, idxs = [], []
    for _ in range(TOP_K):
        mx = jnp.max(work, axis=1, keepdims=True)
        ix = jnp.min(jnp.where(work == mx, lane, LANES), axis=1, keepdims=True)
        vals.append(mx)
        idxs.append(ix)
        work = jnp.where(lane == ix, -jnp.inf, work)
    exps = [jnp.exp(v - vals[0]) for v in vals]
    inv = 1.0 / functools.reduce(jnp.add, exps)
    tm = logits.shape[0]
    onehots = [jnp.where(lane == ix, 1.0, 0.0) for ix in idxs]
    oh_sum = functools.reduce(jnp.add, onehots)
    earlier = lax.broadcasted_iota(I32, (tm, tm), 0) > lax.broadcasted_iota(I32, (tm, tm), 1)
    before = _dot(jnp.where(earlier, 1.0, 0.0).astype(BF16), oh_sum.astype(BF16)) + cnt_s[0:1, :]
    cnt = cnt_s[0:1, :] + jnp.sum(oh_sum, axis=0, keepdims=True)
    cnt_s[...] = jnp.broadcast_to(cnt, cnt_s.shape)
    cnt_ref[...] = jnp.broadcast_to(cnt, cnt_s.shape).astype(I32)
    idx_out = jnp.zeros(logits.shape, I32)
    gate_out = jnp.zeros(logits.shape, F32)
    for k in range(TOP_K):
        rank = jnp.sum(onehots[k] * before, axis=1, keepdims=True).astype(I32)
        idx_out = jnp.where(lane == k, idxs[k], idx_out)
        idx_out = jnp.where(lane == TOP_K + k, rank, idx_out)
        gate_out = jnp.where(lane == k, exps[k] * inv, gate_out)
    idx_ref[...] = idx_out
    gate_ref[...] = gate_out


def _out_proj(x2, mix, w_out, g_ffn, w_router, b_router, E, tm):
    n, d = x2.shape
    return pl.pallas_call(
        functools.partial(_outproj_kernel, E=E),
        out_shape=(
            jax.ShapeDtypeStruct((n, d), F32),
            jax.ShapeDtypeStruct((n, d), F32),
            jax.ShapeDtypeStruct((n, LANES), I32),
            jax.ShapeDtypeStruct((n, LANES), F32),
            jax.ShapeDtypeStruct((8, LANES), I32),
        ),
        grid=(n // tm,),
        in_specs=[
            pl.BlockSpec((tm, d), lambda i: (i, 0)),
            pl.BlockSpec((tm, d), lambda i: (i, 0)),
            pl.BlockSpec((d, d), lambda i: (0, 0)),
            pl.BlockSpec((1, d), lambda i: (0, 0)),
            pl.BlockSpec((d, LANES), lambda i: (0, 0)),
            pl.BlockSpec((1, LANES), lambda i: (0, 0)),
        ],
        out_specs=(
            pl.BlockSpec((tm, d), lambda i: (i, 0)),
            pl.BlockSpec((tm, d), lambda i: (i, 0)),
            pl.BlockSpec((tm, LANES), lambda i: (i, 0)),
            pl.BlockSpec((tm, LANES), lambda i: (i, 0)),
            pl.BlockSpec((8, LANES), lambda i: (0, 0)),
        ),
        scratch_shapes=[pltpu.VMEM((8, LANES), F32)],
        compiler_params=_cparams(("arbitrary",)),
        name="out_proj_router",
    )(x2, mix, w_out, g_ffn, w_router, b_router)


def _route_plan(top_idx, rank, counts, E, P):
    n, k = top_idx.shape
    a = n * k
    e_flat = top_idx.reshape(a)
    padded = ((counts + MOE_SUB - 1) // MOE_SUB) * MOE_SUB
    pend = jnp.cumsum(padded)
    pstart = pend - padded
    onehot = e_flat[:, None] == jnp.arange(E, dtype=I32)[None, :]
    dest = (rank.reshape(a) + jnp.sum(jnp.where(onehot, pstart[None, :], 0), axis=1)).astype(I32)

    nsb = P // MOE_SUB
    spw = MOE_WIN // MOE_SUB
    ni = P // MOE_WIN + E
    sb = jnp.arange(nsb, dtype=I32)
    sb_e = jnp.sum((pend[None, :] <= (sb * MOE_SUB)[:, None]).astype(I32), axis=1)
    sb_e = jnp.minimum(sb_e, E - 1)
    valid = sb * MOE_SUB < pend[-1]
    sb_e = jnp.where(valid, sb_e, sb_e[pend[-1] // MOE_SUB - 1])
    prev_e = jnp.concatenate([jnp.full((1,), -1, I32), sb_e[:-1]])
    is_start = (sb % spw == 0) | (sb_e != prev_e)
    item_of_sb = jnp.cumsum(is_start.astype(I32)) - 1
    n_items = jnp.sum(is_start.astype(I32))
    slot = jnp.where(is_start, item_of_sb, ni)
    item_win = jnp.zeros((ni,), I32).at[slot].set(sb // spw, mode="drop")
    item_e = jnp.zeros((ni,), I32).at[slot].set(sb_e, mode="drop")
    item_lo = jnp.zeros((ni,), I32).at[slot].set(sb % spw, mode="drop")
    win_has_unused = jnp.zeros((nsb // spw,), I32).at[sb // spw].add(1 - valid.astype(I32)) > 0
    zero_win = (sb % spw == 0) & win_has_unused[sb // spw]
    item_first = jnp.zeros((ni,), I32).at[slot].set(zero_win.astype(I32), mode="drop")
    item_len = jnp.zeros((ni,), I32).at[jnp.where(valid, item_of_sb, ni)].add(1, mode="drop")
    it = jnp.arange(ni, dtype=I32)
    live = it < n_items
    last = jnp.maximum(n_items - 1, 0)
    src = jnp.minimum(it, last)
    item_win = item_win[src]
    item_e = item_e[src]
    item_lo = jnp.where(live, item_lo, 0)
    item_hi = jnp.where(live, item_lo + item_len, 0)
    item_first = jnp.where(live, item_first, 0)
    padinfo = jnp.concatenate([pstart + counts, padded - counts, pend[-1:] // MOE_SUB]).astype(I32)
    return dest, padinfo, (item_win, item_e, item_lo, item_hi, item_first)


def _dispatch_kernel(dest_ref, pad_ref, hp_ref, xs_ref, sem, pad_sem, tail_sem, *, G, E):
    base = pl.program_id(0) * G

    def row_copy(r, d, s):
        return pltpu.make_async_copy(hp_ref.at[pl.ds(r, 1)], xs_ref.at[pl.ds(d, 1)], s)

    def issue(r, carry):
        for k in range(TOP_K):
            row_copy(r, dest_ref[(base + r) * TOP_K + k], sem).start(priority=k % 2)
        return carry

    lax.fori_loop(0, G, issue, 0)

    @pl.when(pl.program_id(0) == 0)
    def _():
        def fill(e, carry):
            start = pad_ref[e]

            def one(r, c):
                row_copy(0, start + r, pad_sem).start()
                return c

            return lax.fori_loop(0, pad_ref[E + e], one, carry)

        def drain(e, carry):
            def one(r, c):
                row_copy(0, 0, pad_sem).wait()
                return c

            return lax.fori_loop(0, pad_ref[E + e], one, carry)

        def tail_copy(s):
            rows = pl.ds(pl.multiple_of(s * MOE_SUB, MOE_SUB), MOE_SUB)
            return pltpu.make_async_copy(hp_ref.at[pl.ds(0, MOE_SUB)], xs_ref.at[rows], tail_sem)

        def tail_fill(s, carry):
            tail_copy(s).start()
            return carry

        def tail_drain(s, carry):
            tail_copy(s).wait()
            return carry

        n_sub = xs_ref.shape[0] // MOE_SUB
        lax.fori_loop(0, E, fill, 0)
        lax.fori_loop(pad_ref[2 * E], n_sub, tail_fill, 0)
        lax.fori_loop(0, E, drain, 0)
        lax.fori_loop(pad_ref[2 * E], n_sub, tail_drain, 0)

    pltpu.make_async_copy(xs_ref.at[pl.ds(0, G * TOP_K)], xs_ref.at[pl.ds(0, G * TOP_K)], sem).wait()


def _dispatch(dest, padinfo, hp, P, G, E):
    n, w = hp.shape
    grid_spec = pltpu.PrefetchScalarGridSpec(
        num_scalar_prefetch=2,
        grid=(n // G,),
        in_specs=[pl.BlockSpec((G, w), lambda i, dest, pad: (i, 0))],
        out_specs=pl.BlockSpec(memory_space=pl.ANY),
        scratch_shapes=[pltpu.SemaphoreType.DMA] * 3,
    )
    return pl.pallas_call(
        functools.partial(_dispatch_kernel, G=G, E=E),
        out_shape=jax.ShapeDtypeStruct((P, w), F32),
        grid_spec=grid_spec,
        compiler_params=_cparams(("arbitrary",)),
        name="moe_dispatch",
    )(dest, padinfo, hp)


def _moe_kernel(win_ref, e_ref, lo_ref, hi_ref, first_ref,
                xs_ref, wg_ref, wl_ref, wd_ref, bg_ref, bl_ref, bd_ref, y_ref,
                stage_s, xb_s, sem, *, NI, NJ):
    del e_ref
    w = pl.program_id(0)
    j = pl.program_id(1)
    lo = lo_ref[w]
    hi = hi_ref[w]
    spw = MOE_WIN // MOE_SUB

    def sub_rows(s):
        return pl.ds(pl.multiple_of(s * MOE_SUB, MOE_SUB), MOE_SUB)

    def x_copy(item, s):
        src = pl.ds(pl.multiple_of((win_ref[item] * spw + s) * MOE_SUB, MOE_SUB), MOE_SUB)
        return pltpu.make_async_copy(xs_ref.at[src], stage_s.at[sub_rows(s)], sem)

    def start_item(item):
        def body(s, carry):
            x_copy(item, s).start()
            return carry

        lax.fori_loop(lo_ref[item], hi_ref[item], body, 0)

    @pl.when(j == 0)
    def _():
        @pl.when(w == 0)
        def _():
            start_item(0)

        @pl.when(first_ref[w] == 1)
        def _():
            y_ref[...] = jnp.zeros(y_ref.shape, F32)

        def wait_one(s, carry):
            x_copy(w, s).wait()
            return carry

        lax.fori_loop(lo, hi, wait_one, 0)

    @pl.when(jnp.logical_and(j == 1, w + 1 < NI))
    def _():
        start_item(w + 1)

    def mlp(r0, nrows, first_tile):
        rows = pl.ds(r0, nrows)
        if first_tile:
            xb = stage_s[rows, :].astype(BF16)
            xb_s[rows, :] = xb
        else:
            xb = xb_s[rows, :]
        hg = jnp.minimum(_dot(xb, wg_ref[...]) + bg_ref[...], SWIGLU_LIMIT)
        hl = jnp.clip(_dot(xb, wl_ref[...]) + bl_ref[...], -SWIGLU_LIMIT, SWIGLU_LIMIT)
        act = (hg * _sigmoid(SWIGLU_ALPHA * hg) * (hl + 1.0)).astype(BF16)
        out = _dot(act, wd_ref[...])
        if first_tile:
            y_ref[rows, :] = out + bd_ref[...]
        else:
            y_ref[rows, :] += out

    for g in range(1, spw + 1):
        for first_tile in (True, False):
            @pl.when(jnp.logical_and(hi - lo == g, (j == 0) if first_tile else (j > 0)))
            def _(g=g, first_tile=first_tile):
                done = 0
                while done < g:
                    step = min(2, g - done)
                    mlp(pl.multiple_of((lo + done) * MOE_SUB, MOE_SUB), step * MOE_SUB, first_tile)
                    done += step


def _moe(plan, xs, w_gu, b_gu, w_dn, b_dn):
    P, d = xs.shape
    E, _, f2 = w_gu.shape
    f = f2 // 2
    nj = f // MOE_TF
    assert nj >= 2, "the next item's rows are prefetched at the second hidden tile"
    ni = plan[0].shape[0]
    grid_spec = pltpu.PrefetchScalarGridSpec(
        num_scalar_prefetch=5,
        grid=(ni, nj),
        in_specs=[
            pl.BlockSpec(memory_space=pl.ANY),
            pl.BlockSpec((None, d, MOE_TF), lambda w, j, win, e, lo, hi, fi: (e[w], 0, j)),
            pl.BlockSpec((None, d, MOE_TF), lambda w, j, win, e, lo, hi, fi: (e[w], 0, nj + j)),
            pl.BlockSpec((None, MOE_TF, d), lambda w, j, win, e, lo, hi, fi: (e[w], j, 0)),
            pl.BlockSpec((None, 1, MOE_TF), lambda w, j, win, e, lo, hi, fi: (e[w], 0, j)),
            pl.BlockSpec((None, 1, MOE_TF), lambda w, j, win, e, lo, hi, fi: (e[w], 0, nj + j)),
            pl.BlockSpec((None, 1, d), lambda w, j, win, e, lo, hi, fi: (e[w], 0, 0)),
        ],
        out_specs=pl.BlockSpec((MOE_WIN, d), lambda w, j, win, e, lo, hi, fi: (win[w], 0)),
        scratch_shapes=[
            pltpu.VMEM((MOE_WIN, d), F32),
            pltpu.VMEM((MOE_WIN, d), BF16),
            pltpu.SemaphoreType.DMA,
        ],
    )
    return pl.pallas_call(
        functools.partial(_moe_kernel, NI=ni, NJ=nj),
        out_shape=jax.ShapeDtypeStruct((P, d), F32),
        grid_spec=grid_spec,
        compiler_params=_cparams(("arbitrary", "arbitrary")),
        name="moe_experts",
    )(*plan, xs, w_gu, w_gu, w_dn, b_gu, b_gu, b_dn)


def _final_kernel(dest_ref, x1_ref, gate_ref, p_ref, wple_ref, wpg_ref, gpost_ref, gple_ref, gfin_ref,
                  y_ref, o_ref, yg_s, x3_s, emb_s, xg_s, sem, *, TM, NT, NC, final_norm):
    i = pl.program_id(0)
    slot = i % 2
    nslot = 1 - slot
    nxt = jnp.minimum(i + 1, NT - 1)

    def row_copy(tile, r, k, to_slot):
        d = dest_ref[(tile * TM + r) * TOP_K + k]
        return pltpu.make_async_copy(y_ref.at[pl.ds(d, 1)], yg_s.at[to_slot, k, pl.ds(r, 1)], sem.at[to_slot])

    def wait_slot(s):
        for k in range(TOP_K):
            pltpu.make_async_copy(y_ref.at[pl.ds(0, TM)], yg_s.at[s, k], sem.at[s]).wait()

    @pl.when(i == 0)
    def _():
        def issue(r, carry):
            for k in range(TOP_K):
                row_copy(0, r, k, 0).start(priority=k % 2)
            return carry

        lax.fori_loop(0, TM, issue, 0)

    wait_slot(slot)
    x2 = x1_ref[...]
    for k in range(TOP_K):
        x2 = x2 + gate_ref[:, k:k + 1] * yg_s[slot, k]
    x3_s[...] = x2
    xg_s[...] = _rms(x2, gple_ref[...]).astype(BF16)
    emb_s[...] = _rms(_dot(p_ref[...].astype(BF16), wple_ref[...]), gpost_ref[...])
    cw = x3_s.shape[1] // NC
    per = TM // NC
    for c in range(NC):
        cols = slice(c * cw, (c + 1) * cw)
        pg = _sigmoid(_dot(xg_s[...], wpg_ref[:, cols]))
        x3_s[:, cols] += pg * emb_s[:, cols]
        for r in range(c * per, (c + 1) * per):
            for k in range(TOP_K):
                row_copy(nxt, r, k, nslot).start(priority=k % 2)
    x3 = x3_s[...]
    if final_norm:
        x3 = _rms(x3, gfin_ref[...])
    o_ref[...] = x3

    @pl.when(i == NT - 1)
    def _():
        wait_slot(nslot)


def _final(dest, x1, gates, p2, w_ple, w_pg, g_post, g_ple, g_fin, y, tm, final_norm):
    n, d = x1.shape
    pd = p2.shape[1]
    const = lambda i, dest: (0, 0)
    grid_spec = pltpu.PrefetchScalarGridSpec(
        num_scalar_prefetch=1,
        grid=(n // tm,),
        in_specs=[
            pl.BlockSpec((tm, d), lambda i, dest: (i, 0)),
            pl.BlockSpec((tm, LANES), lambda i, dest: (i, 0)),
            pl.BlockSpec((tm, pd), lambda i, dest: (i, 0)),
            pl.BlockSpec((pd, d), const),
            pl.BlockSpec((d, d), const),
            pl.BlockSpec((1, d), const),
            pl.BlockSpec((1, d), const),
            pl.BlockSpec((1, d), const),
            pl.BlockSpec(memory_space=pl.ANY),
        ],
        out_specs=pl.BlockSpec((tm, d), lambda i, dest: (i, 0)),
        scratch_shapes=[
            pltpu.VMEM((2, TOP_K, tm, d), F32),
            pltpu.VMEM((tm, d), F32),
            pltpu.VMEM((tm, d), F32),
            pltpu.VMEM((tm, d), BF16),
            pltpu.SemaphoreType.DMA((2,)),
        ],
    )
    return pl.pallas_call(
        functools.partial(_final_kernel, TM=tm, NT=n // tm, NC=8, final_norm=final_norm),
        out_shape=jax.ShapeDtypeStruct((n, d), F32),
        grid_spec=grid_spec,
        compiler_params=_cparams(("arbitrary",)),
        name="combine_ple",
    )(dest, x1, gates, p2, w_ple, w_pg, g_post, g_ple, g_fin, y)


def _layer(x2, p2, batch, seq, g_mix, w_in, conv_w, conv_b, b_if, g_mh, g_sgu, w_s, b_s, w_br, w_out,
           g_ffn, w_router, b_router, w_gu, b_gu, w_dn, b_dn, g_ple, w_pg, w_ple, g_ple_post, g_final,
           final_norm):
    n, d = x2.shape
    H = M_HEADS
    E = w_router.shape[1]
    row = lambda v: v.reshape(1, -1).astype(F32)

    c_if = 3 * d
    w_if = jnp.pad(w_in[:, c_if:c_if + 2 * H], ((0, 0), (0, LANES - 2 * H))).astype(BF16)
    bias_if = jnp.pad(b_if.reshape(1, 2 * H), ((0, 0), (0, LANES - 2 * H))).astype(F32)

    tm_a = min(512, n)
    xn, z_if = _input_norm(x2, row(g_mix), w_if, bias_if, tm_a)

    tm = min(1024, n)
    tn = 1024
    n_plain = 2 * d // tn
    sig_tiles = tuple(range(2 * d // tn, 3 * d // tn)) + tuple(range(5 * d // tn, 7 * d // tn))
    f2 = w_gu.shape[2]
    z_main, w_gu16 = _input_proj(xn, w_in.T, w_gu.reshape(E * d, f2), 7 * d, tm, tn, n_plain, c_if // tn,
                                 sig_tiles, 2 * H)

    zif_t = z_if[:, :2 * H].T
    y_a, w_dn16 = _mlstm(z_main, zif_t, conv_w.astype(F32), row(conv_b), row(g_mh),
                         w_dn.reshape(-1, d), batch, seq, d)
    y_b = _sgu(z_main, w_s.astype(F32), b_s.T.astype(F32), row(g_sgu), d, min(512, n))
    mix = _merge(y_a, y_b, z_main, w_br.astype(BF16), d, tm, 512)

    w_r = jnp.pad(w_router, ((0, 0), (0, LANES - E))).astype(F32)
    b_r = jnp.pad(b_router.reshape(1, E), ((0, 0), (0, LANES - E))).astype(F32)
    x1, hp, idx, gates, cnt = _out_proj(x2, mix, w_out.astype(BF16), row(g_ffn), w_r, b_r, E, min(512, n))

    P = n * TOP_K + E * MOE_SUB
    P = -(-P // MOE_WIN) * MOE_WIN
    dest, padinfo, plan = _route_plan(idx[:, :TOP_K], idx[:, TOP_K:2 * TOP_K], cnt[0, :E], E, P)
    xs = _dispatch(dest, padinfo, hp, P, min(1024, n), E)
    y = _moe(plan, xs, w_gu16.reshape(w_gu.shape), b_gu.reshape(E, 1, -1), w_dn16.reshape(w_dn.shape),
             b_dn.reshape(E, 1, -1))
    return _final(dest, x1, gates, p2, w_ple.astype(BF16), w_pg.astype(BF16), row(g_ple_post),
                  row(g_ple), row(g_final), y, min(256, n), final_norm)


def kernel(x, p, g_mix, w_in, conv_w, conv_b, b_if, g_mh, g_sgu, w_s, b_s, w_br, w_out, g_ffn, w_router,
           b_router, w_gu, b_gu, w_dn, b_dn, g_ple, w_pg, w_ple, g_ple_post, g_final):
    batch, seq, d = x.shape
    depth = p.shape[0]
    x2 = x.reshape(batch * seq, d)
    for i in range(depth):
        x2 = _layer(x2, p[i].reshape(batch * seq, -1), batch, seq, g_mix[i], w_in[i], conv_w[i], conv_b[i],
                    b_if[i], g_mh[i], g_sgu[i], w_s[i], b_s[i], w_br[i], w_out[i], g_ffn[i], w_router[i],
                    b_router[i], w_gu[i], b_gu[i], w_dn[i], b_dn[i], g_ple[i], w_pg[i], w_ple[i],
                    g_ple_post[i], g_final, final_norm=(i == depth - 1))
    return x2.reshape(batch, seq, d)
```

```python
import functools

import jax
import jax.numpy as jnp
from jax import lax
from jax.experimental import pallas as pl
from jax.experimental.pallas import tpu as pltpu

F32 = jnp.float32
BF16 = jnp.bfloat16
I32 = jnp.int32

RMS_EPS = 1e-6
M_HEADS = 8
CONV_W = 4
G_GROUPS = 8
G_CHUNK = 128
TOP_K = 4
SWIGLU_ALPHA = 1.702
SWIGLU_LIMIT = 7.0

LANES = 128
MLSTM_CHUNK = 256
MOE_SUB = 256
MOE_WIN = 1024
MOE_TF = 512
VMEM_LIMIT = 56 * 1024 * 1024


def _cparams(sem, vmem=VMEM_LIMIT):
    return pltpu.CompilerParams(dimension_semantics=sem, vmem_limit_bytes=vmem)


def _dot(a, b, **kw):
    return jnp.dot(a, b, preferred_element_type=F32, **kw)


def _rms(x, g):
    ms = jnp.mean(x * x, axis=-1, keepdims=True)
    return x * lax.rsqrt(ms + RMS_EPS) * g


def _sigmoid(x):
    return 0.5 * jnp.tanh(0.5 * x) + 0.5


def _norm_kernel(x_ref, g_ref, wif_ref, bif_ref, xn_ref, zif_ref):
    xn = _rms(x_ref[...], g_ref[...]).astype(BF16)
    xn_ref[...] = xn
    zif_ref[...] = _dot(xn, wif_ref[...]) + bif_ref[...]


def _input_norm(x2, g, w_if, b_if, tm):
    n, d = x2.shape
    return pl.pallas_call(
        _norm_kernel,
        out_shape=(jax.ShapeDtypeStruct((n, d), BF16), jax.ShapeDtypeStruct((n, LANES), F32)),
        grid=(n // tm,),
        in_specs=[
            pl.BlockSpec((tm, d), lambda i: (i, 0)),
            pl.BlockSpec((1, d), lambda i: (0, 0)),
            pl.BlockSpec((d, LANES), lambda i: (0, 0)),
            pl.BlockSpec((1, LANES), lambda i: (0, 0)),
        ],
        out_specs=(pl.BlockSpec((tm, d), lambda i: (i, 0)), pl.BlockSpec((tm, LANES), lambda i: (i, 0))),
        compiler_params=_cparams(("arbitrary",)),
        name="input_norm",
    )(x2, g, w_if, b_if)


def _inproj_kernel(x_ref, wa_ref, wb_ref, cin_ref, o_ref, cout_ref, w_s, *,
                   n_plain, n_aligned, sig_tiles, shift, chunks):
    j = pl.program_id(0)

    @pl.when(pl.program_id(1) == 0)
    def _():
        @pl.when(j < n_aligned)
        def _():
            w_s[...] = wa_ref[...].T.astype(BF16)

        @pl.when(j >= n_aligned)
        def _():
            w_s[...] = jnp.concatenate([wa_ref[shift:, :], wb_ref[...]], axis=0).T.astype(BF16)

    cm = x_ref.shape[0] // chunks

    cc = cin_ref.shape[0] // chunks

    def body(epilogue):
        for r in range(chunks):
            rows = slice(r * cm, (r + 1) * cm)
            o_ref[rows, :] = epilogue(_dot(x_ref[rows, :], w_s[...])).astype(o_ref.dtype)
            crows = slice(r * cc, (r + 1) * cc)
            cout_ref[crows, :] = cin_ref[crows, :].astype(BF16)

    is_sig = functools.reduce(jnp.logical_or, [j == t for t in sig_tiles])
    is_plain = j < n_plain
    pl.when(is_plain)(lambda: body(lambda a: a))
    pl.when(is_sig)(lambda: body(_sigmoid))
    pl.when(jnp.logical_not(jnp.logical_or(is_plain, is_sig)))(lambda: body(jax.nn.gelu))


def _cast_blocks(rows, steps, multiple=16):
    cb = -(-rows // steps)
    cb = -(-cb // multiple) * multiple
    return cb, -(-rows // cb)


def _input_proj(xn, w_in_t, cast_src, c_out, tm, tn, n_plain, n_aligned, sig_tiles, shift):
    n, d = xn.shape
    ni = n // tm
    cast_rows, cast_cols = cast_src.shape
    chunks = max(1, tm // 256)
    cb, ncb = _cast_blocks(cast_rows, (c_out // tn) * ni, 16 * chunks)
    cast_map = lambda j, i: (jnp.minimum(j * ni + i, ncb - 1), 0)
    b_per_tile = tn // shift
    return pl.pallas_call(
        functools.partial(_inproj_kernel, n_plain=n_plain, n_aligned=n_aligned, sig_tiles=sig_tiles,
                          shift=shift, chunks=chunks),
        out_shape=(jax.ShapeDtypeStruct((n, c_out), BF16), jax.ShapeDtypeStruct(cast_src.shape, BF16)),
        grid=(c_out // tn, ni),
        in_specs=[
            pl.BlockSpec((tm, d), lambda j, i: (i, 0)),
            pl.BlockSpec((tn, d), lambda j, i: (j, 0)),
            pl.BlockSpec((shift, d), lambda j, i: (jnp.maximum(j, n_aligned) * b_per_tile + b_per_tile, 0)),
            pl.BlockSpec((cb, cast_cols), cast_map),
        ],
        out_specs=(pl.BlockSpec((tm, tn), lambda j, i: (i, j)), pl.BlockSpec((cb, cast_cols), cast_map)),
        scratch_shapes=[pltpu.VMEM((d, tn), BF16)],
        compiler_params=_cparams(("arbitrary", "arbitrary")),
        name="input_proj",
    )(xn, w_in_t, w_in_t, cast_src)


def _log_sigmoid(x):
    return jnp.minimum(x, 0.0) - jnp.log1p(jnp.exp(-jnp.abs(x)))


def _cumsum_lanes(x):
    n = x.shape[1]
    lane = lax.broadcasted_iota(I32, x.shape, 1)
    sh = 1
    while sh < n:
        x = x + jnp.where(lane >= sh, pltpu.roll(x, sh, axis=1), 0.0)
        sh *= 2
    return x


def _mlstm_kernel(qk_ref, v_ref, o_ref, zr_ref, cw_ref, cb_ref, gmh_ref, cin_ref, y_ref, cout_ref,
                  ext_s, qk_s, c_s, m_s, g_s, *, L, H, DK, DV):
    @pl.when(pl.program_id(1) == 0)
    def _():
        ext_s[0:8, :] = jnp.zeros((8, ext_s.shape[1]), F32)
        c_s[...] = jnp.zeros(c_s.shape, F32)
        m_s[...] = jnp.zeros(m_s.shape, F32)
        g_s[...] = jnp.zeros(g_s.shape, F32)

    u = qk_ref[...].astype(F32)
    ext_s[8:8 + L, :] = u
    cw = cw_ref[...]
    conv = (cb_ref[...] + cw[3:4] * u + cw[2:3] * ext_s[7:7 + L, :]
            + cw[1:2] * ext_s[6:6 + L, :] + cw[0:1] * ext_s[5:5 + L, :])
    ext_s[0:8, :] = ext_s[L:L + 8, :]
    qk_s[...] = conv * _sigmoid(conv)

    zr = zr_ref[...]
    i_rows = zr[0:H]
    b_rows = _cumsum_lanes(_log_sigmoid(zr[H:2 * H]))
    g_s[0:H, :] = b_rows
    b_cols = g_s[...].T

    row = lax.broadcasted_iota(I32, (L, L), 0)
    col = lax.broadcasted_iota(I32, (L, L), 1)
    causal = row >= col
    ones_blk = jnp.where(lax.broadcasted_iota(I32, (L, LANES), 1) == 0, 1.0, 0.0).astype(BF16)
    cc = cin_ref.shape[0] // H

    for h in range(H):
        q = qk_s[:, h * DK:(h + 1) * DK]
        k = qk_s[:, (H + h) * DK:(H + h + 1) * DK] * (DK ** -0.5)
        kt = k.T
        b_col = b_cols[:, h:h + 1]
        b_row = b_rows[h:h + 1, :]
        i_row = i_rows[h:h + 1, :]
        m_prev = m_s[h:h + 1, 0:1]
        dmat = jnp.where(causal, b_col - b_row + i_row, -jnp.inf)
        m_inter = b_col + m_prev
        m_j = jnp.maximum(m_inter, jnp.max(dmat, axis=1, keepdims=True))
        q16 = q.astype(BF16)
        s = _dot(q16, kt.astype(BF16)) * jnp.exp(dmat - m_j)
        inter = jnp.exp(m_inter - m_j)
        vx = jnp.concatenate([v_ref[:, h * DV:(h + 1) * DV], ones_blk], axis=1)
        cx = c_s[h]
        nd = _dot(s.astype(BF16), vx) + inter * _dot(q16, cx.astype(BF16))
        den = nd[:, DV:DV + 1]
        hh = nd[:, 0:DV] / jnp.maximum(jnp.abs(den), jnp.exp(-m_j))
        b_last = b_row[:, L - 1:L]
        w_log = b_last - b_row + i_row
        m_new = jnp.maximum(b_last + m_prev, jnp.max(w_log, axis=1, keepdims=True))
        ktw = (kt * jnp.exp(w_log - m_new)).astype(BF16)
        c_s[h] = jnp.exp(b_last + m_prev - m_new) * cx + _dot(ktw, vx)
        m_s[h:h + 1, :] = jnp.broadcast_to(m_new, (1, LANES))
        yn = _rms(hh, gmh_ref[:, h * DV:(h + 1) * DV])
        y_ref[:, h * DV:(h + 1) * DV] = (yn * o_ref[:, h * DV:(h + 1) * DV].astype(F32)).astype(y_ref.dtype)
        crows = slice(h * cc, (h + 1) * cc)
        cout_ref[crows, :] = cin_ref[crows, :].astype(BF16)


def _mlstm(z_main, zif_t, conv_w, conv_b, g_mh, cast_src, batch, seq, d):
    L, H = MLSTM_CHUNK, M_HEADS
    DV = d // H
    DK = DV // 2
    nc = seq // L
    n = batch * seq
    row_blk = lambda b, c: b * nc + c
    cast_rows, cast_cols = cast_src.shape
    cb, ncb = _cast_blocks(cast_rows, batch * nc, 16 * H)
    cast_map = lambda b, c: (jnp.minimum(row_blk(b, c), ncb - 1), 0)
    return pl.pallas_call(
        functools.partial(_mlstm_kernel, L=L, H=H, DK=DK, DV=DV),
        out_shape=(jax.ShapeDtypeStruct((n, d), BF16), jax.ShapeDtypeStruct(cast_src.shape, BF16)),
        grid=(batch, nc),
        in_specs=[
            pl.BlockSpec((L, d), lambda b, c: (row_blk(b, c), 0)),
            pl.BlockSpec((L, d), lambda b, c: (row_blk(b, c), 1)),
            pl.BlockSpec((L, d), lambda b, c: (row_blk(b, c), 2)),
            pl.BlockSpec((2 * H, L), lambda b, c: (0, row_blk(b, c))),
            pl.BlockSpec((CONV_W, d), lambda b, c: (0, 0)),
            pl.BlockSpec((1, d), lambda b, c: (0, 0)),
            pl.BlockSpec((1, d), lambda b, c: (0, 0)),
            pl.BlockSpec((cb, cast_cols), cast_map),
        ],
        out_specs=(pl.BlockSpec((L, d), lambda b, c: (row_blk(b, c), 0)),
                   pl.BlockSpec((cb, cast_cols), cast_map)),
        scratch_shapes=[
            pltpu.VMEM((L + 8, d), F32),
            pltpu.VMEM((L, d), F32),
            pltpu.VMEM((H, DK, DV + LANES), F32),
            pltpu.VMEM((H, LANES), F32),
            pltpu.VMEM((LANES, L), F32),
        ],
        compiler_params=_cparams(("arbitrary", "arbitrary")),
        name="mlstm",
    )(z_main, z_main, z_main, zif_t, conv_w, conv_b, g_mh, cast_src)


def _sgu_kernel(u_ref, v_ref, ws_ref, bst_ref, g_ref, y_ref, vn_s, *, R, DG):
    vn_s[...] = _rms(v_ref[...].astype(F32), g_ref[...]).astype(BF16)
    row = lax.broadcasted_iota(I32, (G_CHUNK, G_CHUNK), 0)
    col = lax.broadcasted_iota(I32, (G_CHUNK, G_CHUNK), 1)
    for g in range(G_GROUPS):
        ws = jnp.where(row >= col, ws_ref[g], 0.0).astype(BF16)
        bias = bst_ref[:, g:g + 1]
        for c in range(R // G_CHUNK):
            rs = slice(c * G_CHUNK, (c + 1) * G_CHUNK)
            cs = slice(g * DG, (g + 1) * DG)
            vm = _dot(ws, vn_s[rs, cs]) + bias
            y_ref[rs, cs] = (u_ref[rs, cs].astype(F32) * vm).astype(y_ref.dtype)


def _sgu(z_main, w_s, b_s_t, g_sgu, d, R):
    n = z_main.shape[0]
    return pl.pallas_call(
        functools.partial(_sgu_kernel, R=R, DG=d // G_GROUPS),
        out_shape=jax.ShapeDtypeStruct((n, d), BF16),
        grid=(n // R,),
        in_specs=[
            pl.BlockSpec((R, d), lambda i: (i, 3)),
            pl.BlockSpec((R, d), lambda i: (i, 4)),
            pl.BlockSpec((G_GROUPS, G_CHUNK, G_CHUNK), lambda i: (0, 0, 0)),
            pl.BlockSpec((G_CHUNK, G_GROUPS), lambda i: (0, 0)),
            pl.BlockSpec((1, d), lambda i: (0, 0)),
        ],
        out_specs=pl.BlockSpec((R, d), lambda i: (i, 0)),
        scratch_shapes=[pltpu.VMEM((R, d), BF16)],
        compiler_params=_cparams(("arbitrary",)),
        name="spatial_gating",
    )(z_main, z_main, w_s, b_s_t, g_sgu)


def _merge_kernel(ya_ref, yb_ref, ga_ref, gb_ref, w0_ref, w1_ref, o_ref):
    a = _dot(ya_ref[...], w0_ref[...])
    b = _dot(yb_ref[...], w1_ref[...])
    o_ref[...] = (ga_ref[...].astype(F32) * a + gb_ref[...].astype(F32) * b).astype(o_ref.dtype)


def _merge(y_a, y_b, z_main, w_br, d, tm, tn):
    n = y_a.shape[0]
    ga0 = 5 * d // tn
    gb0 = 6 * d // tn
    return pl.pallas_call(
        _merge_kernel,
        out_shape=jax.ShapeDtypeStruct((n, d), BF16),
        grid=(d // tn, n // tm),
        in_specs=[
            pl.BlockSpec((tm, d), lambda j, i: (i, 0)),
            pl.BlockSpec((tm, d), lambda j, i: (i, 0)),
            pl.BlockSpec((tm, tn), lambda j, i: (i, ga0 + j)),
            pl.BlockSpec((tm, tn), lambda j, i: (i, gb0 + j)),
            pl.BlockSpec((None, d, tn), lambda j, i: (0, 0, j)),
            pl.BlockSpec((None, d, tn), lambda j, i: (1, 0, j)),
        ],
        out_specs=pl.BlockSpec((tm, tn), lambda j, i: (i, j)),
        compiler_params=_cparams(("arbitrary", "arbitrary")),
        name="branch_merge",
    )(y_a, y_b, z_main, z_main, w_br, w_br)


def _split_bf16(x):
    hi = x.astype(BF16)
    return hi, (x - hi.astype(F32)).astype(BF16)


def _outproj_kernel(x_ref, mix_ref, wo_ref, gf_ref, wr_ref, br_ref,
                    x1_ref, hp_ref, idx_ref, gate_ref, cnt_ref, cnt_s, *, E):
    @pl.when(pl.program_id(0) == 0)
    def _():
        cnt_s[...] = jnp.zeros(cnt_s.shape, F32)

    x1 = x_ref[...] + _dot(mix_ref[...], wo_ref[...])
    x1_ref[...] = x1
    hn = _rms(x1, gf_ref[...])
    hp_ref[...] = hn
    h_hi, h_lo = _split_bf16(hn)
    w_hi, w_lo = _split_bf16(wr_ref[...])
    logits = _dot(h_hi, w_hi) + (_dot(h_hi, w_lo) + _dot(h_lo, w_hi)) + br_ref[...]
    lane = lax.broadcasted_iota(I32, logits.shape, 1)
    work = jnp.where(lane < E, logits, -jnp.inf)
    vals, idxs = [], []
    for _ in range(TOP_K):
        mx = jnp.max(work, axis=1, keepdims=True)
        ix = jnp.min(jnp.where(work == mx, lane, LANES), axis=1, keepdims=True)
        vals.append(mx)
        idxs.append(ix)
        work = jnp.where(lane == ix, -jnp.inf, work)
    exps = [jnp.exp(v - vals[0]) for v in vals]
    inv = 1.0 / functools.reduce(jnp.add, exps)
    tm = logits.shape[0]
    onehots = [jnp.where(lane == ix, 1.0, 0.0) for ix in idxs]
    oh_sum = functools.reduce(jnp.add, onehots)
    earlier = lax.broadcasted_iota(I32, (tm, tm), 0) > lax.broadcasted_iota(I32, (tm, tm), 1)
    before = _dot(jnp.where(earlier, 1.0, 0.0).astype(BF16), oh_sum.astype(BF16)) + cnt_s[0:1, :]
    cnt = cnt_s[0:1, :] + jnp.sum(oh_sum, axis=0, keepdims=True)
    cnt_s[...] = jnp.broadcast_to(cnt, cnt_s.shape)
    cnt_ref[...] = jnp.broadcast_to(cnt, cnt_s.shape).astype(I32)
    idx_out = jnp.zeros(logits.shape, I32)
    gate_out = jnp.zeros(logits.shape, F32)
    for k in range(TOP_K):
        rank = jnp.sum(onehots[k] * before, axis=1, keepdims=True).astype(I32)
        idx_out = jnp.where(lane == k, idxs[k], idx_out)
        idx_out = jnp.where(lane == TOP_K + k, rank, idx_out)
        gate_out = jnp.where(lane == k, exps[k] * inv, gate_out)
    idx_ref[...] = idx_out
    gate_ref[...] = gate_out


def _out_proj(x2, mix, w_out, g_ffn, w_router, b_router, E, tm):
    n, d = x2.shape
    return pl.pallas_call(
        functools.partial(_outproj_kernel, E=E),
        out_shape=(
            jax.ShapeDtypeStruct((n, d), F32),
            jax.ShapeDtypeStruct((n, d), F32),
            jax.ShapeDtypeStruct((n, LANES), I32),
            jax.ShapeDtypeStruct((n, LANES), F32),
            jax.ShapeDtypeStruct((8, LANES), I32),
        ),
        grid=(n // tm,),
        in_specs=[
            pl.BlockSpec((tm, d), lambda i: (i, 0)),
            pl.BlockSpec((tm, d), lambda i: (i, 0)),
            pl.BlockSpec((d, d), lambda i: (0, 0)),
            pl.BlockSpec((1, d), lambda i: (0, 0)),
            pl.BlockSpec((d, LANES), lambda i: (0, 0)),
            pl.BlockSpec((1, LANES), lambda i: (0, 0)),
        ],
        out_specs=(
            pl.BlockSpec((tm, d), lambda i: (i, 0)),
            pl.BlockSpec((tm, d), lambda i: (i, 0)),
            pl.BlockSpec((tm, LANES), lambda i: (i, 0)),
            pl.BlockSpec((tm, LANES), lambda i: (i, 0)),
            pl.BlockSpec((8, LANES), lambda i: (0, 0)),
        ),
        scratch_shapes=[pltpu.VMEM((8, LANES), F32)],
        compiler_params=_cparams(("arbitrary",)),
        name="out_proj_router",
    )(x2, mix, w_out, g_ffn, w_router, b_router)


def _route_plan(top_idx, rank, counts, E, P):
    n, k = top_idx.shape
    a = n * k
    e_flat = top_idx.reshape(a)
    padded = ((counts + MOE_SUB - 1) // MOE_SUB) * MOE_SUB
    pend = jnp.cumsum(padded)
    pstart = pend - padded
    onehot = e_flat[:, None] == jnp.arange(E, dtype=I32)[None, :]
    dest = (rank.reshape(a) + jnp.sum(jnp.where(onehot, pstart[None, :], 0), axis=1)).astype(I32)

    nsb = P // MOE_SUB
    spw = MOE_WIN // MOE_SUB
    ni = P // MOE_WIN + E
    sb = jnp.arange(nsb, dtype=I32)
    sb_e = jnp.sum((pend[None, :] <= (sb * MOE_SUB)[:, None]).astype(I32), axis=1)
    sb_e = jnp.minimum(sb_e, E - 1)
    valid = sb * MOE_SUB < pend[-1]
    sb_e = jnp.where(valid, sb_e, sb_e[pend[-1] // MOE_SUB - 1])
    prev_e = jnp.concatenate([jnp.full((1,), -1, I32), sb_e[:-1]])
    is_start = (sb % spw == 0) | (sb_e != prev_e)
    item_of_sb = jnp.cumsum(is_start.astype(I32)) - 1
    n_items = jnp.sum(is_start.astype(I32))
    slot = jnp.where(is_start, item_of_sb, ni)
    item_win = jnp.zeros((ni,), I32).at[slot].set(sb // spw, mode="drop")
    item_e = jnp.zeros((ni,), I32).at[slot].set(sb_e, mode="drop")
    item_lo = jnp.zeros((ni,), I32).at[slot].set(sb % spw, mode="drop")
    win_has_unused = jnp.zeros((nsb // spw,), I32).at[sb // spw].add(1 - valid.astype(I32)) > 0
    zero_win = (sb % spw == 0) & win_has_unused[sb // spw]
    item_first = jnp.zeros((ni,), I32).at[slot].set(zero_win.astype(I32), mode="drop")
    item_len = jnp.zeros((ni,), I32).at[jnp.where(valid, item_of_sb, ni)].add(1, mode="drop")
    it = jnp.arange(ni, dtype=I32)
    live = it < n_items
    last = jnp.maximum(n_items - 1, 0)
    src = jnp.minimum(it, last)
    item_win = item_win[src]
    item_e = item_e[src]
    item_lo = jnp.where(live, item_lo, 0)
    item_hi = jnp.where(live, item_lo + item_len, 0)
    item_first = jnp.where(live, item_first, 0)
    padinfo = jnp.concatenate([pstart + counts, padded - counts, pend[-1:] // MOE_SUB]).astype(I32)
    return dest, padinfo, (item_win, item_e, item_lo, item_hi, item_first)


def _dispatch_kernel(dest_ref, pad_ref, hp_ref, xs_ref, sem, pad_sem, tail_sem, *, G, E):
    base = pl.program_id(0) * G

    def row_copy(r, d, s):
        return pltpu.make_async_copy(hp_ref.at[pl.ds(r, 1)], xs_ref.at[pl.ds(d, 1)], s)

    def issue(r, carry):
        for k in range(TOP_K):
            row_copy(r, dest_ref[(base + r) * TOP_K + k], sem).start(priority=k % 2)
        return carry

    lax.fori_loop(0, G, issue, 0)

    @pl.when(pl.program_id(0) == 0)
    def _():
        def fill(e, carry):
            start = pad_ref[e]

            def one(r, c):
                row_copy(0, start + r, pad_sem).start()
                return c

            return lax.fori_loop(0, pad_ref[E + e], one, carry)

        def drain(e, carry):
            def one(r, c):
                row_copy(0, 0, pad_sem).wait()
                return c

            return lax.fori_loop(0, pad_ref[E + e], one, carry)

        def tail_copy(s):
            rows = pl.ds(pl.multiple_of(s * MOE_SUB, MOE_SUB), MOE_SUB)
            return pltpu.make_async_copy(hp_ref.at[pl.ds(0, MOE_SUB)], xs_ref.at[rows], tail_sem)

        def tail_fill(s, carry):
            tail_copy(s).start()
            return carry

        def tail_drain(s, carry):
            tail_copy(s).wait()
            return carry

        n_sub = xs_ref.shape[0] // MOE_SUB
        lax.fori_loop(0, E, fill, 0)
        lax.fori_loop(pad_ref[2 * E], n_sub, tail_fill, 0)
        lax.fori_loop(0, E, drain, 0)
        lax.fori_loop(pad_ref[2 * E], n_sub, tail_drain, 0)

    pltpu.make_async_copy(xs_ref.at[pl.ds(0, G * TOP_K)], xs_ref.at[pl.ds(0, G * TOP_K)], sem).wait()


def _dispatch(dest, padinfo, hp, P, G, E):
    n, w = hp.shape
    grid_spec = pltpu.PrefetchScalarGridSpec(
        num_scalar_prefetch=2,
        grid=(n // G,),
        in_specs=[pl.BlockSpec((G, w), lambda i, dest, pad: (i, 0))],
        out_specs=pl.BlockSpec(memory_space=pl.ANY),
        scratch_shapes=[pltpu.SemaphoreType.DMA] * 3,
    )
    return pl.pallas_call(
        functools.partial(_dispatch_kernel, G=G, E=E),
        out_shape=jax.ShapeDtypeStruct((P, w), F32),
        grid_spec=grid_spec,
        compiler_params=_cparams(("arbitrary",)),
        name="moe_dispatch",
    )(dest, padinfo, hp)


def _moe_kernel(win_ref, e_ref, lo_ref, hi_ref, first_ref,
                xs_ref, wg_ref, wl_ref, wd_ref, bg_ref, bl_ref, bd_ref, y_ref,
                stage_s, xb_s, sem, *, NI, NJ):
    del e_ref
    w = pl.program_id(0)
    j = pl.program_id(1)
    lo = lo_ref[w]
    hi = hi_ref[w]
    spw = MOE_WIN // MOE_SUB

    def sub_rows(s):
        return pl.ds(pl.multiple_of(s * MOE_SUB, MOE_SUB), MOE_SUB)

    def x_copy(item, s):
        src = pl.ds(pl.multiple_of((win_ref[item] * spw + s) * MOE_SUB, MOE_SUB), MOE_SUB)
        return pltpu.make_async_copy(xs_ref.at[src], stage_s.at[sub_rows(s)], sem)

    def start_item(item):
        def body(s, carry):
            x_copy(item, s).start()
            return carry

        lax.fori_loop(lo_ref[item], hi_ref[item], body, 0)

    @pl.when(j == 0)
    def _():
        @pl.when(w == 0)
        def _():
            start_item(0)

        @pl.when(first_ref[w] == 1)
        def _():
            y_ref[...] = jnp.zeros(y_ref.shape, F32)

        def wait_one(s, carry):
            x_copy(w, s).wait()
            return carry

        lax.fori_loop(lo, hi, wait_one, 0)

    @pl.when(jnp.logical_and(j == 1, w + 1 < NI))
    def _():
        start_item(w + 1)

    def mlp(r0, nrows, first_tile):
        rows = pl.ds(r0, nrows)
        if first_tile:
            xb = stage_s[rows, :].astype(BF16)
            xb_s[rows, :] = xb
        else:
            xb = xb_s[rows, :]
        hg = jnp.minimum(_dot(xb, wg_ref[...]) + bg_ref[...], SWIGLU_LIMIT)
        hl = jnp.clip(_dot(xb, wl_ref[...]) + bl_ref[...], -SWIGLU_LIMIT, SWIGLU_LIMIT)
        act = (hg * _sigmoid(SWIGLU_ALPHA * hg) * (hl + 1.0)).astype(BF16)
        out = _dot(act, wd_ref[...])
        if first_tile:
            y_ref[rows, :] = out + bd_ref[...]
        else:
            y_ref[rows, :] += out

    for g in range(1, spw + 1):
        for first_tile in (True, False):
            @pl.when(jnp.logical_and(hi - lo == g, (j == 0) if first_tile else (j > 0)))
            def _(g=g, first_tile=first_tile):
                done = 0
                while done < g:
                    step = min(2, g - done)
                    mlp(pl.multiple_of((lo + done) * MOE_SUB, MOE_SUB), step * MOE_SUB, first_tile)
                    done += step


def _moe(plan, xs, w_gu, b_gu, w_dn, b_dn):
    P, d = xs.shape
    E, _, f2 = w_gu.shape
    f = f2 // 2
    nj = f // MOE_TF
    assert nj >= 2, "the next item's rows are prefetched at the second hidden tile"
    ni = plan[0].shape[0]
    grid_spec = pltpu.PrefetchScalarGridSpec(
        num_scalar_prefetch=5,
        grid=(ni, nj),
        in_specs=[
            pl.BlockSpec(memory_space=pl.ANY),
            pl.BlockSpec((None, d, MOE_TF), lambda w, j, win, e, lo, hi, fi: (e[w], 0, j)),
            pl.BlockSpec((None, d, MOE_TF), lambda w, j, win, e, lo, hi, fi: (e[w], 0, nj + j)),
            pl.BlockSpec((None, MOE_TF, d), lambda w, j, win, e, lo, hi, fi: (e[w], j, 0)),
            pl.BlockSpec((None, 1, MOE_TF), lambda w, j, win, e, lo, hi, fi: (e[w], 0, j)),
            pl.BlockSpec((None, 1, MOE_TF), lambda w, j, win, e, lo, hi, fi: (e[w], 0, nj + j)),
            pl.BlockSpec((None, 1, d), lambda w, j, win, e, lo, hi, fi: (e[w], 0, 0)),
        ],
        out_specs=pl.BlockSpec((MOE_WIN, d), lambda w, j, win, e, lo, hi, fi: (win[w], 0)),
        scratch_shapes=[
            pltpu.VMEM((MOE_WIN, d), F32),
            pltpu.VMEM((MOE_WIN, d), BF16),
            pltpu.SemaphoreType.DMA,
        ],
    )
    return pl.pallas_call(
        functools.partial(_moe_kernel, NI=ni, NJ=nj),
        out_shape=jax.ShapeDtypeStruct((P, d), F32),
        grid_spec=grid_spec,
        compiler_params=_cparams(("arbitrary", "arbitrary")),
        name="moe_experts",
    )(*plan, xs, w_gu, w_gu, w_dn, b_gu, b_gu, b_dn)


def _final_kernel(dest_ref, x1_ref, gate_ref, p_ref, wple_ref, wpg_ref, gpost_ref, gple_ref, gfin_ref,
                  y_ref, o_ref, yg_s, x3_s, emb_s, xg_s, sem, *, TM, NT, NC, final_norm):
    i = pl.program_id(0)
    slot = i % 2
    nslot = 1 - slot
    nxt = jnp.minimum(i + 1, NT - 1)

    def row_copy(tile, r, k, to_slot):
        d = dest_ref[(tile * TM + r) * TOP_K + k]
        return pltpu.make_async_copy(y_ref.at[pl.ds(d, 1)], yg_s.at[to_slot, k, pl.ds(r, 1)], sem.at[to_slot])

    def wait_slot(s):
        for k in range(TOP_K):
            pltpu.make_async_copy(y_ref.at[pl.ds(0, TM)], yg_s.at[s, k], sem.at[s]).wait()

    @pl.when(i == 0)
    def _():
        def issue(r, carry):
            for k in range(TOP_K):
                row_copy(0, r, k, 0).start(priority=k % 2)
            return carry

        lax.fori_loop(0, TM, issue, 0)

    wait_slot(slot)
    x2 = x1_ref[...]
    for k in range(TOP_K):
        x2 = x2 + gate_ref[:, k:k + 1] * yg_s[slot, k]
    x3_s[...] = x2
    xg_s[...] = _rms(x2, gple_ref[...]).astype(BF16)
    emb_s[...] = _rms(_dot(p_ref[...].astype(BF16), wple_ref[...]), gpost_ref[...])
    cw = x3_s.shape[1] // NC
    issue_chunks = NC - 2
    per = -(-TM // issue_chunks)
    for c in range(NC):
        cols = slice(c * cw, (c + 1) * cw)
        pg = _sigmoid(_dot(xg_s[...], wpg_ref[:, cols]))
        x3_s[:, cols] += pg * emb_s[:, cols]
        for r in range(min(c * per, TM), min((c + 1) * per, TM)):
            for k in range(TOP_K):
                row_copy(nxt, r, k, nslot).start(priority=k % 2)
    x3 = x3_s[...]
    if final_norm:
        x3 = _rms(x3, gfin_ref[...])
    o_ref[...] = x3

    @pl.when(i == NT - 1)
    def _():
        wait_slot(nslot)


def _final(dest, x1, gates, p2, w_ple, w_pg, g_post, g_ple, g_fin, y, tm, final_norm):
    n, d = x1.shape
    pd = p2.shape[1]
    const = lambda i, dest: (0, 0)
    grid_spec = pltpu.PrefetchScalarGridSpec(
        num_scalar_prefetch=1,
        grid=(n // tm,),
        in_specs=[
            pl.BlockSpec((tm, d), lambda i, dest: (i, 0)),
            pl.BlockSpec((tm, LANES), lambda i, dest: (i, 0)),
            pl.BlockSpec((tm, pd), lambda i, dest: (i, 0)),
            pl.BlockSpec((pd, d), const),
            pl.BlockSpec((d, d), const),
            pl.BlockSpec((1, d), const),
            pl.BlockSpec((1, d), const),
            pl.BlockSpec((1, d), const),
            pl.BlockSpec(memory_space=pl.ANY),
        ],
        out_specs=pl.BlockSpec((tm, d), lambda i, dest: (i, 0)),
        scratch_shapes=[
            pltpu.VMEM((2, TOP_K, tm, d), F32),
            pltpu.VMEM((tm, d), F32),
            pltpu.VMEM((tm, d), F32),
            pltpu.VMEM((tm, d), BF16),
            pltpu.SemaphoreType.DMA((2,)),
        ],
    )
    return pl.pallas_call(
        functools.partial(_final_kernel, TM=tm, NT=n // tm, NC=8, final_norm=final_norm),
        out_shape=jax.ShapeDtypeStruct((n, d), F32),
        grid_spec=grid_spec,
        compiler_params=_cparams(("arbitrary",)),
        name="combine_ple",
    )(dest, x1, gates, p2, w_ple, w_pg, g_post, g_ple, g_fin, y)


def _layer(x2, p2, batch, seq, g_mix, w_in, conv_w, conv_b, b_if, g_mh, g_sgu, w_s, b_s, w_br, w_out,
           g_ffn, w_router, b_router, w_gu, b_gu, w_dn, b_dn, g_ple, w_pg, w_ple, g_ple_post, g_final,
           final_norm):
    n, d = x2.shape
    H = M_HEADS
    E = w_router.shape[1]
    row = lambda v: v.reshape(1, -1).astype(F32)

    c_if = 3 * d
    w_if = jnp.pad(w_in[:, c_if:c_if + 2 * H], ((0, 0), (0, LANES - 2 * H))).astype(BF16)
    bias_if = jnp.pad(b_if.reshape(1, 2 * H), ((0, 0), (0, LANES - 2 * H))).astype(F32)

    tm_a = min(512, n)
    xn, z_if = _input_norm(x2, row(g_mix), w_if, bias_if, tm_a)

    tm = min(1024, n)
    tn = 1024
    n_plain = 2 * d // tn
    sig_tiles = tuple(range(2 * d // tn, 3 * d // tn)) + tuple(range(5 * d // tn, 7 * d // tn))
    f2 = w_gu.shape[2]
    z_main, w_gu16 = _input_proj(xn, w_in.T, w_gu.reshape(E * d, f2), 7 * d, tm, tn, n_plain, c_if // tn,
                                 sig_tiles, 2 * H)

    zif_t = z_if[:, :2 * H].T
    y_a, w_dn16 = _mlstm(z_main, zif_t, conv_w.astype(F32), row(conv_b), row(g_mh),
                         w_dn.reshape(-1, d), batch, seq, d)
    y_b = _sgu(z_main, w_s.astype(F32), b_s.T.astype(F32), row(g_sgu), d, min(512, n))
    mix = _merge(y_a, y_b, z_main, w_br.astype(BF16), d, tm, 512)

    w_r = jnp.pad(w_router, ((0, 0), (0, LANES - E))).astype(F32)
    b_r = jnp.pad(b_router.reshape(1, E), ((0, 0), (0, LANES - E))).astype(F32)
    x1, hp, idx, gates, cnt = _out_proj(x2, mix, w_out.astype(BF16), row(g_ffn), w_r, b_r, E, min(512, n))

    P = n * TOP_K + E * MOE_SUB
    P = -(-P // MOE_WIN) * MOE_WIN
    dest, padinfo, plan = _route_plan(idx[:, :TOP_K], idx[:, TOP_K:2 * TOP_K], cnt[0, :E], E, P)
    xs = _dispatch(dest, padinfo, hp, P, min(1024, n), E)
    y = _moe(plan, xs, w_gu16.reshape(w_gu.shape), b_gu.reshape(E, 1, -1), w_dn16.reshape(w_dn.shape),
             b_dn.reshape(E, 1, -1))
    return _final(dest, x1, gates, p2, w_ple.astype(BF16), w_pg.astype(BF16), row(g_ple_post),
                  row(g_ple), row(g_final), y, min(256, n), final_norm)


def kernel(x, p, g_mix, w_in, conv_w, conv_b, b_if, g_mh, g_sgu, w_s, b_s, w_br, w_out, g_ffn, w_router,
           b_router, w_gu, b_gu, w_dn, b_dn, g_ple, w_pg, w_ple, g_ple_post, g_final):
    batch, seq, d = x.shape
    depth = p.shape[0]
    x2 = x.reshape(batch * seq, d)
    for i in range(depth):
        x2 = _layer(x2, p[i].reshape(batch * seq, -1), batch, seq, g_mix[i], w_in[i], conv_w[i], conv_b[i],
                    b_if[i], g_mh[i], g_sgu[i], w_s[i], b_s[i], w_br[i], w_out[i], g_ffn[i], w_router[i],
                    b_router[i], w_gu[i], b_gu[i], w_dn[i], b_dn[i], g_ple[i], w_pg[i], w_ple[i],
                    g_ple_post[i], g_final, final_norm=(i == depth - 1))
    return x2.reshape(batch, seq, d)
```
